```python
import math
import jax, jax.numpy as jnp
from jax import lax
import numpy as np

D_MODEL = 1024
BATCH = 4
SEQ = 8192
DEPTH = 4

MEM_TOKENS = 256
N_MIXERS = 2
MIX_WIDTH = 2 * D_MODEL
XATTN_WIDTH = MIX_WIDTH // 4
TOKEN_WIDTH = MIX_WIDTH - XATTN_WIDTH
XATTN_HEADS = 4
XATTN_HEAD_DIM = XATTN_WIDTH // XATTN_HEADS
S5_GROUP = 16
S5_STATE = 64
S5_GROUPS = TOKEN_WIDTH // S5_GROUP
GDN_HEAD_DIM = 128
GDN_V_HEADS = TOKEN_WIDTH // GDN_HEAD_DIM
GDN_QK_HEADS = GDN_V_HEADS // 2
GDN_QK_WIDTH = GDN_QK_HEADS * GDN_HEAD_DIM
CONV_WIDTH = 4
CHUNK = 64
NORM_EPS = 1e-6
S5_IN = 2 * TOKEN_WIDTH + 2 * XATTN_WIDTH
GDN_IN = 2 * GDN_QK_WIDTH + TOKEN_WIDTH + 2 * GDN_V_HEADS + TOKEN_WIDTH + 2 * XATTN_WIDTH

kernel_name = "hybrid_s5_gdn_xattn_trunk"


def rmsnorm(x, w):
    xf = x.astype(jnp.float32)
    xf = xf * lax.rsqrt(jnp.mean(xf * xf, axis=-1, keepdims=True) + NORM_EPS)
    return xf.astype(x.dtype) * w


def l2norm(x):
    xf = x.astype(jnp.float32)
    return xf * lax.rsqrt(jnp.sum(xf * xf, axis=-1, keepdims=True) + NORM_EPS)


def causal_conv(x, w):
    k = w.shape[0]
    length = x.shape[1]
    xp = jnp.pad(x, ((0, 0), (k - 1, 0), (0, 0)))
    return sum(xp[:, j:j + length] * w[j] for j in range(k))


def cross_attn(q, mem_h, w_kv):
    bsz, length, _ = q.shape
    kv = mem_h @ w_kv
    k, v = jnp.split(kv, 2, axis=-1)
    q = q.reshape(bsz, length, XATTN_HEADS, XATTN_HEAD_DIM)
    k = k.reshape(bsz, -1, XATTN_HEADS, XATTN_HEAD_DIM)
    v = v.reshape(bsz, -1, XATTN_HEADS, XATTN_HEAD_DIM)
    s = jnp.einsum('blhd,bmhd->bhlm', q, k).astype(jnp.float32) * (XATTN_HEAD_DIM ** -0.5)
    p = jax.nn.softmax(s, axis=-1).astype(v.dtype)
    o = jnp.einsum('bhlm,bmhd->blhd', p, v)
    return o.reshape(bsz, length, XATTN_WIDTH)


def _s5_combine(c1, c2):
    a1r, a1i, b1r, b1i = c1
    a2r, a2i, b2r, b2i = c2
    ar = a1r * a2r - a1i * a2i
    ai = a1r * a2i + a1i * a2r
    br = a2r * b1r - a2i * b1i + b2r
    bi = a2r * b1i + a2i * b1r + b2i
    return (ar, ai, br, bi)


def s5_mixer(u, lam_re, lam_im, log_step, b_re, b_im, c_re, c_im, d, w_glu, b_glu):
    bsz, length, _ = u.shape
    f32 = jnp.float32
    uf = u.astype(f32)
    ug = uf.reshape(bsz, length, S5_GROUPS, S5_GROUP)
    dt = jnp.exp(log_step.astype(f32))[:, None]
    lr, li = lam_re.astype(f32), lam_im.astype(f32)
    mag = jnp.exp(lr * dt)
    ar, ai = mag * jnp.cos(li * dt), mag * jnp.sin(li * dt)
    den = lr * lr + li * li
    nr, ni = ar - 1.0, ai
    cr, ci = (nr * lr + ni * li) / den, (ni * lr - nr * li) / den
    br, bi = b_re.astype(f32), b_im.astype(f32)
    bbar_re = cr[..., None] * br - ci[..., None] * bi
    bbar_im = cr[..., None] * bi + ci[..., None] * br
    bu_re = jnp.einsum('blgh,gph->blgp', ug, bbar_re)
    bu_im = jnp.einsum('blgh,gph->blgp', ug, bbar_im)
    a_re = jnp.broadcast_to(ar, (1, length, S5_GROUPS, S5_STATE))
    a_im = jnp.broadcast_to(ai, (1, length, S5_GROUPS, S5_STATE))
    _, _, x_re, x_im = lax.associative_scan(_s5_combine, (a_re, a_im, bu_re, bu_im), axis=1)
    y = (jnp.einsum('blgp,ghp->blgh', x_re, c_re.astype(f32))
         - jnp.einsum('blgp,ghp->blgh', x_im, c_im.astype(f32)))
    y = y.reshape(bsz, length, TOKEN_WIDTH) + d.astype(f32) * uf
    y = jax.nn.gelu(y)
    y = y * jax.nn.sigmoid(y @ w_glu.astype(f32) + b_glu.astype(f32))
    return y.astype(u.dtype)


def gated_delta_chunked(q, k, v, g, beta):
    f32 = jnp.float32
    bsz, length, heads, dk = k.shape
    dv = v.shape[-1]
    n = length // CHUNK

    def to_chunks(t):
        return t.astype(f32).reshape(bsz, n, CHUNK, heads, -1).transpose(1, 0, 3, 2, 4)

    q, k, v = to_chunks(q), to_chunks(k), to_chunks(v)
    g = g.astype(f32).reshape(bsz, n, CHUNK, heads).transpose(1, 0, 3, 2)
    beta = beta.astype(f32).reshape(bsz, n, CHUNK, heads).transpose(1, 0, 3, 2)
    g = jnp.cumsum(g, axis=-1)
    k_beta = k * beta[..., None]
    v_beta = v * beta[..., None]
    causal = jnp.tril(jnp.ones((CHUNK, CHUNK), dtype=bool))
    strict = jnp.tril(jnp.ones((CHUNK, CHUNK), dtype=bool), -1)
    decay = jnp.exp(jnp.where(causal, g[..., :, None] - g[..., None, :], -jnp.inf))
    l_mat = jnp.where(strict, jnp.einsum('nbhid,nbhjd->nbhij', k_beta, k) * decay, 0.0)
    eye = jnp.broadcast_to(jnp.eye(CHUNK, dtype=f32), l_mat.shape)
    t_mat = lax.linalg.triangular_solve(l_mat, eye, left_side=True, lower=True, unit_diagonal=True)
    u = jnp.einsum('nbhij,nbhjd->nbhid', t_mat, v_beta)
    w = jnp.einsum('nbhij,nbhjd->nbhid', t_mat, k_beta * jnp.exp(g)[..., None])
    intra = jnp.where(causal, jnp.einsum('nbhid,nbhjd->nbhij', q, k) * decay, 0.0)

    def step(state, xs):
        q_c, k_c, u_c, w_c, g_c, a_c = xs
        v_new = u_c - jnp.einsum('bhcd,bhde->bhce', w_c, state)
        o = (jnp.einsum('bhcd,bhde->bhce', q_c * jnp.exp(g_c)[..., None], state)
             + jnp.einsum('bhij,bhje->bhie', a_c, v_new))
        g_last = g_c[..., -1]
        k_dec = k_c * jnp.exp(g_last[..., None] - g_c)[..., None]
        state = state * jnp.exp(g_last)[..., None, None] + jnp.einsum('bhcd,bhce->bhde', k_dec, v_new)
        return state, o

    s0 = jnp.zeros((bsz, heads, dk, dv), dtype=f32)
    _, o = lax.scan(step, s0, (q, k, u, w, g, intra))
    return o.transpose(1, 0, 3, 2, 4).reshape(bsz, length, heads, dv)


def gdn_mixer(qkv, a, b, gate, conv_w, a_log, dt_bias, norm_w):
    bsz, length, _ = qkv.shape
    f32 = jnp.float32
    qkv = jax.nn.silu(causal_conv(qkv, conv_w))
    q, k, v = jnp.split(qkv, [GDN_QK_WIDTH, 2 * GDN_QK_WIDTH], axis=-1)
    rep = GDN_V_HEADS // GDN_QK_HEADS
    q = jnp.repeat(l2norm(q.reshape(bsz, length, GDN_QK_HEADS, GDN_HEAD_DIM)), rep, axis=2) * (GDN_HEAD_DIM ** -0.5)
    k = jnp.repeat(l2norm(k.reshape(bsz, length, GDN_QK_HEADS, GDN_HEAD_DIM)), rep, axis=2)
    v = v.reshape(bsz, length, GDN_V_HEADS, GDN_HEAD_DIM)
    g = -jnp.exp(a_log.astype(f32)) * jax.nn.softplus(a.astype(f32) + dt_bias.astype(f32))
    beta = jax.nn.sigmoid(b.astype(f32))
    o = gated_delta_chunked(q, k, v, g, beta).astype(qkv.dtype)
    o = rmsnorm(o, norm_w).reshape(bsz, length, TOKEN_WIDTH)
    return o * jax.nn.silu(gate)


def setup_inputs(seed: int = 0) -> dict:
    key = jax.random.key(seed)
    ks = jax.random.split(key, 24)
    n_s5 = (DEPTH + 1) // 2
    n_gdn = DEPTH // 2
    nrm = jax.random.normal
    f32 = jnp.float32
    x = nrm(ks[0], (BATCH, SEQ, D_MODEL), f32)
    mem = nrm(ks[1], (BATCH, MEM_TOKENS, D_MODEL), f32)
    norm_w = 1.0 + 0.02 * nrm(ks[2], (DEPTH, D_MODEL), f32)
    w_out = nrm(ks[3], (DEPTH, MIX_WIDTH, D_MODEL), f32) * MIX_WIDTH ** -0.5
    mem_norm_w = 1.0 + 0.02 * nrm(ks[4], (DEPTH, D_MODEL), f32)
    w_mem_kv = nrm(ks[5], (DEPTH, D_MODEL, 2 * XATTN_WIDTH), f32) * D_MODEL ** -0.5
    s5_w_in = nrm(ks[6], (n_s5, D_MODEL, S5_IN), f32) * D_MODEL ** -0.5
    s5_lambda_re = -0.5 + 0.01 * nrm(ks[7], (n_s5, S5_GROUPS, S5_STATE), f32)
    s5_lambda_im = (math.pi * jnp.arange(S5_STATE, dtype=f32)
                    + 0.01 * nrm(ks[8], (n_s5, S5_GROUPS, S5_STATE), f32))
    s5_log_step = jax.random.uniform(ks[9], (n_s5, S5_GROUPS), f32, math.log(1e-3), math.log(1e-1))
    s5_b_re = nrm(ks[10], (n_s5, S5_GROUPS, S5_STATE, S5_GROUP), f32) * (2 * S5_GROUP) ** -0.5
    s5_b_im = nrm(ks[11], (n_s5, S5_GROUPS, S5_STATE, S5_GROUP), f32) * (2 * S5_GROUP) ** -0.5
    s5_c_re = nrm(ks[12], (n_s5, S5_GROUPS, S5_GROUP, S5_STATE), f32) * S5_STATE ** -0.5
    s5_c_im = nrm(ks[13], (n_s5, S5_GROUPS, S5_GROUP, S5_STATE), f32) * S5_STATE ** -0.5
    s5_d = nrm(ks[14], (n_s5, TOKEN_WIDTH), f32)
    s5_w_glu = nrm(ks[15], (n_s5, TOKEN_WIDTH, TOKEN_WIDTH), f32) * TOKEN_WIDTH ** -0.5
    s5_b_glu = 0.01 * nrm(ks[16], (n_s5, TOKEN_WIDTH), f32)
    gdn_w_in = nrm(ks[17], (n_gdn, D_MODEL, GDN_IN), f32) * D_MODEL ** -0.5
    gdn_conv_w = nrm(ks[18], (n_gdn, CONV_WIDTH, 2 * GDN_QK_WIDTH + TOKEN_WIDTH), f32) * CONV_WIDTH ** -0.5
    gdn_a_log = jnp.log(jax.random.uniform(ks[19], (n_gdn, GDN_V_HEADS), f32, 1.0, 16.0))
    dt = jnp.exp(jax.random.uniform(ks[20], (n_gdn, GDN_V_HEADS), f32, math.log(1e-3), math.log(1e-1)))
    gdn_dt_bias = dt + jnp.log(-jnp.expm1(-dt))
    gdn_norm_w = 1.0 + 0.02 * nrm(ks[21], (n_gdn, GDN_HEAD_DIM), f32)
    final_norm_w = 1.0 + 0.02 * nrm(ks[22], (D_MODEL,), f32)
    return {"x": x, "mem": mem, "norm_w": norm_w, "w_out": w_out, "mem_norm_w": mem_norm_w,
            "w_mem_kv": w_mem_kv, "s5_w_in": s5_w_in, "s5_lambda_re": s5_lambda_re,
            "s5_lambda_im": s5_lambda_im, "s5_log_step": s5_log_step, "s5_b_re": s5_b_re,
            "s5_b_im": s5_b_im, "s5_c_re": s5_c_re, "s5_c_im": s5_c_im, "s5_d": s5_d,
            "s5_w_glu": s5_w_glu, "s5_b_glu": s5_b_glu, "gdn_w_in": gdn_w_in,
            "gdn_conv_w": gdn_conv_w, "gdn_a_log": gdn_a_log, "gdn_dt_bias": gdn_dt_bias,
            "gdn_norm_w": gdn_norm_w, "final_norm_w": final_norm_w}


def reference(x, mem, norm_w, w_out, mem_norm_w, w_mem_kv, s5_w_in, s5_lambda_re, s5_lambda_im,
              s5_log_step, s5_b_re, s5_b_im, s5_c_re, s5_c_im, s5_d, s5_w_glu, s5_b_glu,
              gdn_w_in, gdn_conv_w, gdn_a_log, gdn_dt_bias, gdn_norm_w, final_norm_w):
    for i in range(DEPTH):
        j = i // N_MIXERS
        h = rmsnorm(x, norm_w[i])
        mem_h = rmsnorm(mem, mem_norm_w[i])
        if i % N_MIXERS == 0:
            proj = h @ s5_w_in[j]
            u, gate_mix, q_x, gate_x = jnp.split(
                proj, [TOKEN_WIDTH, 2 * TOKEN_WIDTH, 2 * TOKEN_WIDTH + XATTN_WIDTH], axis=-1)
            y_mix = s5_mixer(u, s5_lambda_re[j], s5_lambda_im[j], s5_log_step[j], s5_b_re[j],
                             s5_b_im[j], s5_c_re[j], s5_c_im[j], s5_d[j], s5_w_glu[j],
                             s5_b_glu[j]) * jax.nn.silu(gate_mix)
        else:
            proj = h @ gdn_w_in[j]
            o1 = 2 * GDN_QK_WIDTH + TOKEN_WIDTH
            o2 = o1 + GDN_V_HEADS
            o3 = o2 + GDN_V_HEADS
            o4 = o3 + TOKEN_WIDTH
            o5 = o4 + XATTN_WIDTH
            qkv, a, b, gate_mix, q_x, gate_x = jnp.split(proj, [o1, o2, o3, o4, o5], axis=-1)
            y_mix = gdn_mixer(qkv, a, b, gate_mix, gdn_conv_w[j], gdn_a_log[j], gdn_dt_bias[j],
                              gdn_norm_w[j])
        y_x = cross_attn(q_x, mem_h, w_mem_kv[i]) * jax.nn.silu(gate_x)
        x = x + jnp.concatenate([y_mix, y_x], axis=-1) @ w_out[i]
    return rmsnorm(x, final_norm_w)
```

```python
import functools
import math

import jax
import jax.numpy as jnp
from jax import lax
from jax.experimental import pallas as pl
from jax.experimental.pallas import tpu as pltpu

F32 = jnp.float32
BF16 = jnp.bfloat16

NORM_EPS = 1e-6
XATTN_HEADS = 4
S5_GROUP = 16
S5_STATE = 64
GDN_HEAD_DIM = 128
CONV_WIDTH = 4
CHUNK = 64

LANES = 128
SUBLANES = 8
VMEM_LIMIT = 56 * 1024 * 1024

S5_GROUPS_PER_BLOCK = 16
S5_BLOCK_IN = S5_GROUPS_PER_BLOCK * S5_GROUP
S5_BLOCK_HALF = S5_GROUPS_PER_BLOCK * S5_STATE
S5_PAIRS = S5_BLOCK_HALF // LANES


def _dot(a, b):
    return jnp.dot(a, b, preferred_element_type=F32)


def _dot_nt(a, b):
    return lax.dot_general(a, b, (((1,), (1,)), ((), ())), preferred_element_type=F32)


def _dot_tn(a, b):
    return lax.dot_general(a, b, (((0,), (0,)), ((), ())), preferred_element_type=F32)


def _dot_f32(a, b):
    return jnp.dot(a, b, preferred_element_type=F32, precision=lax.Precision.HIGHEST)


def _sigmoid(x):
    return 1.0 / (1.0 + jnp.exp(-x))


def _silu(x):
    return x * _sigmoid(x)


def _gelu_tanh(x):
    c = math.sqrt(2.0 / math.pi)
    return 0.5 * x * (1.0 + jnp.tanh(c * (x + 0.044715 * (x * x * x))))


def _cparams(sem):
    return pltpu.CompilerParams(dimension_semantics=sem, vmem_limit_bytes=VMEM_LIMIT)


def _norm_proj_kernel(x_ref, nw_ref, w_ref, *o_refs, col_splits, blocked):
    x = x_ref[...]
    h = x * lax.rsqrt(jnp.mean(x * x, axis=-1, keepdims=True) + NORM_EPS)
    hb = (h * nw_ref[...]).astype(BF16)
    for o_ref, (c0, c1), blk in zip(o_refs, col_splits, blocked):
        if blk:
            for i in range((c1 - c0) // blk):
                o_ref[i] = _dot(hb, w_ref[:, c0 + i * blk:c0 + (i + 1) * blk]).astype(o_ref.dtype)
        else:
            step = 512
            for a in range(c0, c1, step):
                b = min(a + step, c1)
                o_ref[:, a - c0:b - c0] = _dot(hb, w_ref[:, a:b]).astype(o_ref.dtype)


def _norm_proj(x2d, nw, w_bf16, outs, tm):
    t, d = x2d.shape
    n = w_bf16.shape[1]
    out_shapes, out_specs = [], []
    for c0, c1, dt, blk in outs:
        if blk:
            nb = (c1 - c0) // blk
            out_shapes.append(jax.ShapeDtypeStruct((nb, t, blk), dt))
            out_specs.append(pl.BlockSpec((nb, tm, blk), lambda i: (0, i, 0)))
        else:
            out_shapes.append(jax.ShapeDtypeStruct((t, c1 - c0), dt))
            out_specs.append(pl.BlockSpec((tm, c1 - c0), lambda i: (i, 0)))
    kern = functools.partial(_norm_proj_kernel,
                             col_splits=tuple((o[0], o[1]) for o in outs),
                             blocked=tuple(o[3] for o in outs))
    return pl.pallas_call(
        kern,
        grid=(t // tm,),
        in_specs=[pl.BlockSpec((tm, d), lambda i: (i, 0)),
                  pl.BlockSpec((1, d), lambda i: (0, 0)),
                  pl.BlockSpec((d, n), lambda i: (0, 0))],
        out_specs=out_specs,
        out_shape=out_shapes,
        compiler_params=_cparams(("parallel",)),
        name="norm_proj",
    )(x2d, nw.reshape(1, d), w_bf16)


def _cmul(ar, ai, xr, xi):
    return ar * xr - ai * xi, ar * xi + ai * xr


def _shift_rows(x, k, rows):
    return jnp.where(rows >= k, pltpu.roll(x, k, axis=0), 0.0)


def _s5_kernel(u_ref, gate_ref, bbd_ref, cbd_ref, ltab_ref, lpow_ref, d_ref, wglu_ref, bglu_ref,
               o_ref, xre_scr, xim_scr, carry_scr, y_scr, *, tl, nblk):
    sub = tl // SUBLANES
    t_idx = pl.program_id(1)

    @pl.when(t_idx == 0)
    def _():
        carry_scr[...] = jnp.zeros_like(carry_scr)

    rows = lax.broadcasted_iota(jnp.int32, (SUBLANES, LANES), 0)

    def block_body(cb, _):
        bu = _dot(u_ref[cb], bbd_ref[cb])
        for n in range(S5_PAIRS):
            xre_scr[n] = bu[:, n * LANES:(n + 1) * LANES]
            xim_scr[n] = bu[:, S5_BLOCK_HALF + n * LANES:S5_BLOCK_HALF + (n + 1) * LANES]

        for grp in range(0, S5_PAIRS, 4):
            pairs = list(range(grp, grp + 4))
            lam = [(ltab_ref[cb, 0, :, n * LANES:(n + 1) * LANES],
                    ltab_ref[cb, 1, :, n * LANES:(n + 1) * LANES]) for n in pairs]
            st = [(jnp.zeros((SUBLANES, LANES), F32), jnp.zeros((SUBLANES, LANES), F32)) for _ in pairs]
            for j in range(sub):
                for q, n in enumerate(pairs):
                    ar, ai = lam[q]
                    xr, xi = st[q]
                    idx = (pl.ds(j, SUBLANES, stride=sub), slice(None))
                    re_n, im_n = xre_scr.at[n], xim_scr.at[n]
                    pr, pi = _cmul(ar, ai, xr, xi)
                    xr = pr + re_n[idx]
                    xi = pi + im_n[idx]
                    re_n[idx] = xr
                    im_n[idx] = xi
                    st[q] = (xr, xi)
            xin = []
            for q, n in enumerate(pairs):
                ls = slice(n * LANES, (n + 1) * LANES)
                er, ei = st[q]
                fr = pltpu.roll(jnp.where(rows == SUBLANES - 1, carry_scr[cb, 0, :, ls], er), 1, axis=0)
                fi = pltpu.roll(jnp.where(rows == SUBLANES - 1, carry_scr[cb, 1, :, ls], ei), 1, axis=0)
                for lvl, k in enumerate((1, 2, 4)):
                    pr_, pi_ = ltab_ref[cb, 2 + 2 * lvl, :, ls], ltab_ref[cb, 3 + 2 * lvl, :, ls]
                    sr, si = _shift_rows(fr, k, rows), _shift_rows(fi, k, rows)
                    mr, mi = _cmul(pr_, pi_, sr, si)
                    fr, fi = fr + mr, fi + mi
                ajr, aji = ltab_ref[cb, 2, :, ls], ltab_ref[cb, 3, :, ls]
                mr, mi = _cmul(ajr, aji, fr, fi)
                carry_scr[cb, 0, :, ls] = mr + er
                carry_scr[cb, 1, :, ls] = mi + ei
                xin.append((fr, fi))
            for j in range(sub):
                for q, n in enumerate(pairs):
                    ls = slice(n * LANES, (n + 1) * LANES)
                    fr, fi = xin[q]
                    pr_ = lpow_ref[cb, 0, pl.ds(j, 1), ls]
                    pi_ = lpow_ref[cb, 1, pl.ds(j, 1), ls]
                    idx = (pl.ds(j, SUBLANES, stride=sub), slice(None))
                    re_n, im_n = xre_scr.at[n], xim_scr.at[n]
                    mr, mi = _cmul(pr_, pi_, fr, fi)
                    re_n[idx] = re_n[idx] + mr
                    im_n[idx] = im_n[idx] + mi

        xs = jnp.concatenate([xre_scr[n] for n in range(S5_PAIRS)]
                             + [xim_scr[n] for n in range(S5_PAIRS)], axis=1).astype(BF16)
        y_scr[cb] = _dot(xs, cbd_ref[cb])
        return 0

    lax.fori_loop(0, nblk, block_body, 0)

    y = jnp.concatenate([y_scr[i] for i in range(nblk)], axis=1)
    u = jnp.concatenate([u_ref[i] for i in range(nblk)], axis=1).astype(F32)
    y = _gelu_tanh(y + d_ref[...] * u)
    z = _dot(y.astype(BF16), wglu_ref[...]) + bglu_ref[...]
    y = y * _sigmoid(z)
    o_ref[...] = (y * _silu(gate_ref[...].astype(F32))).astype(o_ref.dtype)


def _s5_tables(lam_re, lam_im, log_step, b_re, b_im, c_re, c_im, tl):
    groups = lam_re.shape[0]
    nblk = groups // S5_GROUPS_PER_BLOCK
    sub = tl // SUBLANES
    dt = jnp.exp(log_step.astype(F32))[:, None]
    lr, li = lam_re.astype(F32), lam_im.astype(F32)
    mag = jnp.exp(lr * dt)
    ar, ai = mag * jnp.cos(li * dt), mag * jnp.sin(li * dt)
    den = lr * lr + li * li
    nr, ni = ar - 1.0, ai
    cr, ci = (nr * lr + ni * li) / den, (ni * lr - nr * li) / den
    br, bi = b_re.astype(F32), b_im.astype(F32)
    bbar_re = cr[..., None] * br - ci[..., None] * bi
    bbar_im = cr[..., None] * bi + ci[..., None] * br

    gb = S5_GROUPS_PER_BLOCK
    eye = jnp.eye(gb, dtype=F32)

    def expand_blockdiag(w):
        w = w.reshape(nblk, gb, S5_STATE, S5_GROUP)
        return jnp.einsum('ngph,gk->nghkp', w, eye).reshape(nblk, gb * S5_GROUP, gb * S5_STATE)

    bbd = jnp.concatenate([expand_blockdiag(bbar_re), expand_blockdiag(bbar_im)], axis=2).astype(BF16)

    def contract_blockdiag(w):
        w = w.reshape(nblk, gb, S5_GROUP, S5_STATE)
        return jnp.einsum('nghp,gk->ngpkh', w, eye).reshape(nblk, gb * S5_STATE, gb * S5_GROUP)

    cbd = jnp.concatenate([contract_blockdiag(c_re.astype(F32)),
                           -contract_blockdiag(c_im.astype(F32))], axis=1).astype(BF16)

    def lam_pow(k):
        kk = jnp.asarray(k, F32)
        m = jnp.exp(lr * dt * kk)
        return ((m * jnp.cos(li * dt * kk)).reshape(nblk, gb * S5_STATE),
                (m * jnp.sin(li * dt * kk)).reshape(nblk, gb * S5_STATE))

    rows = []
    for k in (1, sub, 2 * sub, 4 * sub):
        pr, pi = lam_pow(k)
        rows += [pr, pi]
    ltab = jnp.stack(rows, axis=1)
    ltab = jnp.broadcast_to(ltab[:, :, None, :], (nblk, 8, SUBLANES, gb * S5_STATE))
    steps = jnp.arange(1, sub + 1, dtype=F32)[None, :, None]
    lrd = (lr * dt).reshape(nblk, 1, gb * S5_STATE)
    lid = (li * dt).reshape(nblk, 1, gb * S5_STATE)
    m = jnp.exp(lrd * steps)
    lpow = jnp.stack([m * jnp.cos(lid * steps), m * jnp.sin(lid * steps)], axis=1)
    return bbd, cbd, ltab, lpow


def _s5_mixer(u_blk, gate, tables, d, w_glu, b_glu, bsz, length, tl):
    bbd, cbd, ltab, lpow = tables
    nblk = u_blk.shape[0]
    width = nblk * S5_BLOCK_IN
    nt = length // tl
    kern = functools.partial(_s5_kernel, tl=tl, nblk=nblk)
    const = lambda *shape: pl.BlockSpec(shape, lambda b, t: (0,) * len(shape))
    return pl.pallas_call(
        kern,
        grid=(bsz, nt),
        in_specs=[pl.BlockSpec((nblk, tl, S5_BLOCK_IN), lambda b, t: (0, b * nt + t, 0)),
                  pl.BlockSpec((tl, width), lambda b, t: (b * nt + t, 0)),
                  const(*bbd.shape), const(*cbd.shape), const(*ltab.shape), const(*lpow.shape),
                  const(1, width), const(width, width), const(1, width)],
        out_specs=pl.BlockSpec((tl, width), lambda b, t: (b * nt + t, 0)),
        out_shape=jax.ShapeDtypeStruct((bsz * length, width), BF16),
        scratch_shapes=[pltpu.VMEM((S5_PAIRS, tl, LANES), F32),
                        pltpu.VMEM((S5_PAIRS, tl, LANES), F32),
                        pltpu.VMEM((nblk, 2, SUBLANES, S5_BLOCK_HALF), F32),
                        pltpu.VMEM((nblk, tl, S5_BLOCK_IN), F32)],
        compiler_params=_cparams(("parallel", "arbitrary")),
        name="s5_mixer",
    )(u_blk, gate, bbd, cbd, ltab, lpow, d.reshape(1, width).astype(F32),
      w_glu.astype(BF16), b_glu.reshape(1, width).astype(F32))


def _gdn_kernel(qkv_ref, ab_ref, gate_ref, convw_ref, gpar_ref, normw_ref, o_ref,
                ext_scr, q_scr, k_scr, v_scr, g_scr, beta_scr, state_scr, *, tl, qk_heads, v_heads):
    hd = GDN_HEAD_DIM
    qkw = qk_heads * hd
    rep = v_heads // qk_heads
    t_idx = pl.program_id(1)

    @pl.when(t_idx == 0)
    def _():
        state_scr[...] = jnp.zeros_like(state_scr)
        ext_scr[0:SUBLANES, :] = jnp.zeros((SUBLANES, ext_scr.shape[1]), F32)

    xp = qkv_ref[...].astype(F32)
    ext_scr[SUBLANES:SUBLANES + tl, :] = xp
    conv = convw_ref[0:1, :] * ext_scr[pl.ds(SUBLANES - 3, tl), :]
    for j in range(1, CONV_WIDTH):
        conv = conv + convw_ref[j:j + 1, :] * ext_scr[pl.ds(SUBLANES - 3 + j, tl), :]
    ext_scr[0:SUBLANES, :] = xp[tl - SUBLANES:tl, :]
    xc = _silu(conv)

    for h in range(qk_heads):
        qh = xc[:, h * hd:(h + 1) * hd]
        kh = xc[:, qkw + h * hd:qkw + (h + 1) * hd]
        qn = qh * lax.rsqrt(jnp.sum(qh * qh, axis=-1, keepdims=True) + NORM_EPS) * (hd ** -0.5)
        kn = kh * lax.rsqrt(jnp.sum(kh * kh, axis=-1, keepdims=True) + NORM_EPS)
        q_scr[:, h * hd:(h + 1) * hd] = qn
        k_scr[:, h * hd:(h + 1) * hd] = kn
    v_scr[...] = xc[:, 2 * qkw:]

    ab = ab_ref[...]
    xg = ab + gpar_ref[1:2, :]
    softplus = jnp.maximum(xg, 0.0) + jnp.log(1.0 + jnp.exp(-jnp.abs(xg)))
    g_scr[...] = gpar_ref[0:1, :] * softplus
    beta_scr[...] = _sigmoid(ab)

    ci = lax.broadcasted_iota(jnp.int32, (CHUNK, CHUNK), 0)
    cj = lax.broadcasted_iota(jnp.int32, (CHUNK, CHUNK), 1)
    causal = ci >= cj
    strict = ci > cj
    tril_f = causal.astype(F32)
    eye = (ci == cj).astype(F32)
    level_masks = []
    for b in (1, 2, 4, 8, 16, 32):
        level_masks.append(((ci // (2 * b)) == (cj // (2 * b))) & ((ci % (2 * b)) >= b) & ((cj % (2 * b)) < b))

    def chunk_body(c, _):
        r0 = pl.multiple_of(c * CHUNK, CHUNK)
        rs = pl.ds(r0, CHUNK)
        gc = _dot_f32(tril_f, g_scr[rs, :])
        gct = gc.T
        beta_t = beta_scr[rs, :]
        eg = jnp.exp(gc)
        for hq in range(qk_heads):
            qh = q_scr[rs, hq * hd:(hq + 1) * hd]
            kh = k_scr[rs, hq * hd:(hq + 1) * hd]
            kb16 = kh.astype(BF16)
            kq = _dot_nt(jnp.concatenate([kh, qh], axis=0).astype(BF16), kb16)
            kk, qk = kq[:CHUNK], kq[CHUNK:]
            for hv in range(hq * rep, (hq + 1) * rep):
                g_col = gc[:, hv:hv + 1]
                g_row = gct[hv:hv + 1, :]
                beta = beta_t[:, v_heads + hv:v_heads + hv + 1]
                decay = jnp.exp(jnp.where(causal, g_col - g_row, -jnp.inf))
                lmat = jnp.where(strict, beta * kk * decay, 0.0)
                x = eye - jnp.where(level_masks[0], lmat, 0.0)
                for m in level_masks[1:]:
                    cm = jnp.where(m, lmat, 0.0)
                    x = x - _dot_f32(_dot_f32(x, cm), x)
                eg_col = eg[:, hv:hv + 1]
                vh = v_scr[rs, hv * hd:(hv + 1) * hd]
                rhs = jnp.concatenate([vh * beta, kh * (beta * eg_col)], axis=1).astype(BF16)
                uw = _dot(x.astype(BF16), rhs)
                u_c, w_c = uw[:, :hd], uw[:, hd:]
                s = state_scr[hv]
                s16 = s.astype(BF16)
                ws = _dot(jnp.concatenate([w_c, qh * eg_col], axis=0).astype(BF16), s16)
                v_new = u_c - ws[:CHUNK]
                intra = jnp.where(causal, qk * decay, 0.0)
                o = ws[CHUNK:] + _dot(intra.astype(BF16), v_new.astype(BF16))
                g_last = gc[CHUNK - 1:CHUNK, hv:hv + 1]
                k_dec = kh * jnp.exp(g_last - g_col)
                state_scr[hv] = s * jnp.exp(g_last) + _dot_tn(k_dec.astype(BF16), v_new.astype(BF16))
                o = o * lax.rsqrt(jnp.mean(o * o, axis=-1, keepdims=True) + NORM_EPS) * normw_ref[...]
                gt = gate_ref[rs, hv * hd:(hv + 1) * hd].astype(F32)
                o_ref[rs, hv * hd:(hv + 1) * hd] = (o * _silu(gt)).astype(o_ref.dtype)
        return 0

    lax.fori_loop(0, tl // CHUNK, chunk_body, 0)


def _gdn_mixer(qkv, ab, gate, conv_w, a_log, dt_bias, norm_w, bsz, length, tl):
    v_heads = a_log.shape[0]
    qk_heads = v_heads // 2
    cw = qkv.shape[1]
    vw = v_heads * GDN_HEAD_DIM
    qkw = qk_heads * GDN_HEAD_DIM
    nt = length // tl
    gpar = jnp.zeros((2, LANES), F32)
    gpar = gpar.at[0, :v_heads].set(-jnp.exp(a_log.astype(F32)))
    gpar = gpar.at[1, :v_heads].set(dt_bias.astype(F32))
    kern = functools.partial(_gdn_kernel, tl=tl, qk_heads=qk_heads, v_heads=v_heads)
    const = lambda *shape: pl.BlockSpec(shape, lambda b, t: (0,) * len(shape))
    return pl.pallas_call(
        kern,
        grid=(bsz, nt),
        in_specs=[pl.BlockSpec((tl, cw), lambda b, t: (b * nt + t, 0)),
                  pl.BlockSpec((tl, LANES), lambda b, t: (b * nt + t, 0)),
                  pl.BlockSpec((tl, vw), lambda b, t: (b * nt + t, 0)),
                  const(CONV_WIDTH, cw), const(2, LANES), const(1, GDN_HEAD_DIM)],
        out_specs=pl.BlockSpec((tl, vw), lambda b, t: (b * nt + t, 0)),
        out_shape=jax.ShapeDtypeStruct((bsz * length, vw), BF16),
        scratch_shapes=[pltpu.VMEM((tl + SUBLANES, cw), F32),
                        pltpu.VMEM((tl, qkw), F32),
                        pltpu.VMEM((tl, qkw), F32),
                        pltpu.VMEM((tl, vw), F32),
                        pltpu.VMEM((tl, LANES), F32),
                        pltpu.VMEM((tl, LANES), F32),
                        pltpu.VMEM((v_heads, GDN_HEAD_DIM, GDN_HEAD_DIM), F32)],
        compiler_params=_cparams(("parallel", "arbitrary")),
        name="gdn_mixer",
    )(qkv, ab, gate, conv_w.astype(F32), gpar, norm_w.reshape(1, GDN_HEAD_DIM).astype(F32))


def _attn_out_kernel(ymix_ref, qx_ref, gx_ref, k_ref, v_ref, wa_ref, wb_ref, x_ref, fw_ref, o_ref,
                     *, final_norm):
    xw = qx_ref.shape[1]
    hd = xw // XATTN_HEADS
    q = qx_ref[...]
    ys = []
    for h in range(XATTN_HEADS):
        hs = slice(h * hd, (h + 1) * hd)
        s = _dot_nt(q[:, hs], k_ref[:, hs]) * (hd ** -0.5)
        s = s - jnp.max(s, axis=-1, keepdims=True)
        p = jnp.exp(s)
        p = p / jnp.sum(p, axis=-1, keepdims=True)
        ys.append(_dot(p.astype(BF16), v_ref[:, hs]))
    yx = jnp.concatenate(ys, axis=1) * _silu(gx_ref[...].astype(F32))
    acc = _dot(ymix_ref[...], wa_ref[...]) + _dot(yx.astype(BF16), wb_ref[...])
    xn = x_ref[...] + acc
    if final_norm:
        xn = xn * lax.rsqrt(jnp.mean(xn * xn, axis=-1, keepdims=True) + NORM_EPS) * fw_ref[...]
    o_ref[...] = xn


def _attn_out(ymix, qx, gx, k, v, w_out, x2d, final_w, bsz, length, tm, final_norm):
    t, d = x2d.shape
    mw = ymix.shape[1]
    xw = qx.shape[1]
    mem = k.shape[0] // bsz
    nt = length // tm
    wa = w_out[:mw].astype(BF16)
    wb = w_out[mw:].astype(BF16)
    kern = functools.partial(_attn_out_kernel, final_norm=final_norm)
    row = lambda width: pl.BlockSpec((tm, width), lambda b, i: (b * nt + i, 0))
    const = lambda *shape: pl.BlockSpec(shape, lambda b, i: (0,) * len(shape))
    return pl.pallas_call(
        kern,
        grid=(bsz, nt),
        in_specs=[row(mw), row(xw), row(xw),
                  pl.BlockSpec((mem, xw), lambda b, i: (b, 0)),
                  pl.BlockSpec((mem, xw), lambda b, i: (b, 0)),
                  const(mw, d), const(xw, d), row(d), const(1, d)],
        out_specs=row(d),
        out_shape=jax.ShapeDtypeStruct((t, d), F32),
        compiler_params=_cparams(("parallel", "parallel")),
        name="attn_out",
    )(ymix, qx, gx, k, v, wa, wb, x2d, final_w.reshape(1, d).astype(F32))


def _pick_tile(n, pref):
    t = pref
    while n % t:
        t //= 2
    return t


def kernel(x, mem, norm_w, w_out, mem_norm_w, w_mem_kv, s5_w_in, s5_lambda_re, s5_lambda_im, s5_log_step, s5_b_re, s5_b_im, s5_c_re, s5_c_im, s5_d, s5_w_glu, s5_b_glu, gdn_w_in, gdn_conv_w, gdn_a_log, gdn_dt_bias, gdn_norm_w, final_norm_w):
    bsz, length, d = x.shape
    mem_tokens = mem.shape[1]
    depth = norm_w.shape[0]
    mix_width = w_out.shape[1]
    xw = mix_width // 4
    tw = mix_width - xw
    v_heads = gdn_a_log.shape[1]
    qkw = (v_heads // 2) * GDN_HEAD_DIM

    t = bsz * length
    x2d = x.reshape(t, d)
    mem2d = mem.reshape(bsz * mem_tokens, d)
    tm_proj = _pick_tile(t, 512)
    tm_mem = _pick_tile(bsz * mem_tokens, 256)
    tl_s5 = _pick_tile(length, 256)
    tl_gdn = _pick_tile(length, 256)
    tm_out = _pick_tile(length, 512)

    for i in range(depth):
        j = i // 2
        kmem, vmem = _norm_proj(mem2d, mem_norm_w[i], w_mem_kv[i].astype(BF16),
                                [(0, xw, BF16, 0), (xw, 2 * xw, BF16, 0)], tm_mem)
        if i % 2 == 0:
            u_blk, gate_mix, qx, gx = _norm_proj(
                x2d, norm_w[i], s5_w_in[j].astype(BF16),
                [(0, tw, BF16, S5_BLOCK_IN), (tw, 2 * tw, BF16, 0),
                 (2 * tw, 2 * tw + xw, BF16, 0), (2 * tw + xw, 2 * tw + 2 * xw, BF16, 0)], tm_proj)
            tables = _s5_tables(s5_lambda_re[j], s5_lambda_im[j], s5_log_step[j], s5_b_re[j], s5_b_im[j],
                                s5_c_re[j], s5_c_im[j], tl_s5)
            ymix = _s5_mixer(u_blk, gate_mix, tables, s5_d[j], s5_w_glu[j], s5_b_glu[j], bsz, length, tl_s5)
        else:
            w = gdn_w_in[j]
            o1 = 2 * qkw + tw
            o3 = o1 + 2 * v_heads
            o4 = o3 + tw
            ab_w = jnp.zeros((d, LANES), w.dtype).at[:, :2 * v_heads].set(w[:, o1:o3])
            wp = jnp.concatenate([w[:, :o1], w[:, o3:], ab_w], axis=1).astype(BF16)
            c_gate = o1
            c_qx = c_gate + tw
            c_gx = c_qx + xw
            c_ab = c_gx + xw
            qkv, gate_mix, qx, gx, ab = _norm_proj(
                x2d, norm_w[i], wp,
                [(0, o1, BF16, 0), (c_gate, c_qx, BF16, 0), (c_qx, c_gx, BF16, 0),
                 (c_gx, c_ab, BF16, 0), (c_ab, c_ab + LANES, F32, 0)], tm_proj)
            ymix = _gdn_mixer(qkv, ab, gate_mix, gdn_conv_w[j], gdn_a_log[j], gdn_dt_bias[j],
                              gdn_norm_w[j], bsz, length, tl_gdn)
        x2d = _attn_out(ymix, qx, gx, kmem, vmem, w_out[i], x2d, final_norm_w, bsz, length, tm_out,
                        final_norm=(i == depth - 1))
    return x2d.reshape(bsz, length, d)
```

```python
import functools
import math

import jax
import jax.numpy as jnp
from jax import lax
from jax.experimental import pallas as pl
from jax.experimental.pallas import tpu as pltpu

F32 = jnp.float32
BF16 = jnp.bfloat16

NORM_EPS = 1e-6
XATTN_HEADS = 4
S5_GROUP = 16
S5_STATE = 64
GDN_HEAD_DIM = 128
CONV_WIDTH = 4
CHUNK = 64

LANES = 128
SUBLANES = 8
VMEM_LIMIT = 56 * 1024 * 1024

S5_GROUPS_PER_BLOCK = 16
S5_BLOCK_IN = S5_GROUPS_PER_BLOCK * S5_GROUP
S5_BLOCK_HALF = S5_GROUPS_PER_BLOCK * S5_STATE
S5_PAIRS = S5_BLOCK_HALF // LANES


def _dot(a, b):
    return jnp.dot(a, b, preferred_element_type=F32)


def _dot_nt(a, b):
    return lax.dot_general(a, b, (((1,), (1,)), ((), ())), preferred_element_type=F32)


def _dot_tn(a, b):
    return lax.dot_general(a, b, (((0,), (0,)), ((), ())), preferred_element_type=F32)


def _dot_f32(a, b):
    return jnp.dot(a, b, preferred_element_type=F32, precision=lax.Precision.HIGHEST)


def _sigmoid(x):
    return 1.0 / (1.0 + jnp.exp(-x))


def _silu(x):
    return x * _sigmoid(x)


def _gelu_tanh(x):
    c = math.sqrt(2.0 / math.pi)
    return 0.5 * x * (1.0 + jnp.tanh(c * (x + 0.044715 * (x * x * x))))


def _cparams(sem):
    return pltpu.CompilerParams(dimension_semantics=sem, vmem_limit_bytes=VMEM_LIMIT)


def _norm_proj_kernel(x_ref, nw_ref, w_ref, *o_refs, col_splits, blocked):
    x = x_ref[...]
    h = x * lax.rsqrt(jnp.mean(x * x, axis=-1, keepdims=True) + NORM_EPS)
    hb = (h * nw_ref[...]).astype(BF16)
    for o_ref, (c0, c1), blk in zip(o_refs, col_splits, blocked):
        if blk:
            for i in range((c1 - c0) // blk):
                o_ref[i] = _dot(hb, w_ref[:, c0 + i * blk:c0 + (i + 1) * blk]).astype(o_ref.dtype)
        else:
            step = 512
            for a in range(c0, c1, step):
                b = min(a + step, c1)
                o_ref[:, a - c0:b - c0] = _dot(hb, w_ref[:, a:b]).astype(o_ref.dtype)


def _norm_proj(x2d, nw, w_bf16, outs, tm):
    t, d = x2d.shape
    n = w_bf16.shape[1]
    out_shapes, out_specs = [], []
    for c0, c1, dt, blk in outs:
        if blk:
            nb = (c1 - c0) // blk
            out_shapes.append(jax.ShapeDtypeStruct((nb, t, blk), dt))
            out_specs.append(pl.BlockSpec((nb, tm, blk), lambda i: (0, i, 0)))
        else:
            out_shapes.append(jax.ShapeDtypeStruct((t, c1 - c0), dt))
            out_specs.append(pl.BlockSpec((tm, c1 - c0), lambda i: (i, 0)))
    kern = functools.partial(_norm_proj_kernel,
                             col_splits=tuple((o[0], o[1]) for o in outs),
                             blocked=tuple(o[3] for o in outs))
    return pl.pallas_call(
        kern,
        grid=(t // tm,),
        in_specs=[pl.BlockSpec((tm, d), lambda i: (i, 0)),
                  pl.BlockSpec((1, d), lambda i: (0, 0)),
                  pl.BlockSpec((d, n), lambda i: (0, 0))],
        out_specs=out_specs,
        out_shape=out_shapes,
        compiler_params=_cparams(("parallel",)),
        name="norm_proj",
    )(x2d, nw.reshape(1, d), w_bf16)


def _cmul(ar, ai, xr, xi):
    return ar * xr - ai * xi, ar * xi + ai * xr


def _shift_rows(x, k, rows):
    return jnp.where(rows >= k, pltpu.roll(x, k, axis=0), 0.0)


def _s5_kernel(u_ref, gate_ref, perm_ref, permt_ref, bbd_ref, cbd_ref, ltab_ref, d_ref, wglu_ref, bglu_ref,
               o_ref, xre_scr, xim_scr, carry_scr, y_scr, up_scr, *, tl, nblk):
    sub = tl // SUBLANES
    t_idx = pl.program_id(1)

    @pl.when(t_idx == 0)
    def _():
        carry_scr[...] = jnp.zeros_like(carry_scr)

    rows = lax.broadcasted_iota(jnp.int32, (SUBLANES, LANES), 0)
    perm = perm_ref[...]

    def block_body(cb, _):
        up = _dot(perm, u_ref[cb]).astype(BF16)
        up_scr[cb] = up
        bu = _dot(up, bbd_ref[cb])
        for n in range(S5_PAIRS):
            xre_scr[n] = bu[:, n * LANES:(n + 1) * LANES]
            xim_scr[n] = bu[:, S5_BLOCK_HALF + n * LANES:S5_BLOCK_HALF + (n + 1) * LANES]

        for grp in range(0, S5_PAIRS, 4):
            pairs = list(range(grp, grp + 4))
            lam = [(ltab_ref[cb, 0, :, n * LANES:(n + 1) * LANES],
                    ltab_ref[cb, 1, :, n * LANES:(n + 1) * LANES]) for n in pairs]
            st = [(jnp.zeros((SUBLANES, LANES), F32), jnp.zeros((SUBLANES, LANES), F32)) for _ in pairs]
            for j in range(sub):
                for q, n in enumerate(pairs):
                    ar, ai = lam[q]
                    xr, xi = st[q]
                    pr, pi = _cmul(ar, ai, xr, xi)
                    st[q] = (pr + xre_scr[n, pl.ds(j * SUBLANES, SUBLANES), :],
                             pi + xim_scr[n, pl.ds(j * SUBLANES, SUBLANES), :])
            for q, n in enumerate(pairs):
                ls = slice(n * LANES, (n + 1) * LANES)
                er, ei = st[q]
                fr = pltpu.roll(jnp.where(rows == SUBLANES - 1, carry_scr[cb, 0, :, ls], er), 1, axis=0)
                fi = pltpu.roll(jnp.where(rows == SUBLANES - 1, carry_scr[cb, 1, :, ls], ei), 1, axis=0)
                for lvl, k in enumerate((1, 2, 4)):
                    pr_, pi_ = ltab_ref[cb, 2 + 2 * lvl, :, ls], ltab_ref[cb, 3 + 2 * lvl, :, ls]
                    sr, si = _shift_rows(fr, k, rows), _shift_rows(fi, k, rows)
                    mr, mi = _cmul(pr_, pi_, sr, si)
                    fr, fi = fr + mr, fi + mi
                st[q] = (fr, fi)
            for j in range(sub):
                for q, n in enumerate(pairs):
                    ar, ai = lam[q]
                    xr, xi = st[q]
                    pr, pi = _cmul(ar, ai, xr, xi)
                    xr = pr + xre_scr[n, pl.ds(j * SUBLANES, SUBLANES), :]
                    xi = pi + xim_scr[n, pl.ds(j * SUBLANES, SUBLANES), :]
                    xre_scr[n, pl.ds(j * SUBLANES, SUBLANES), :] = xr
                    xim_scr[n, pl.ds(j * SUBLANES, SUBLANES), :] = xi
                    st[q] = (xr, xi)
            for q, n in enumerate(pairs):
                ls = slice(n * LANES, (n + 1) * LANES)
                carry_scr[cb, 0, :, ls] = st[q][0]
                carry_scr[cb, 1, :, ls] = st[q][1]

        xs = jnp.concatenate([xre_scr[n] for n in range(S5_PAIRS)]
                             + [xim_scr[n] for n in range(S5_PAIRS)], axis=1).astype(BF16)
        y_scr[cb] = _dot(xs, cbd_ref[cb])
        return 0

    lax.fori_loop(0, nblk, block_body, 0)

    y = jnp.concatenate([y_scr[i] for i in range(nblk)], axis=1)
    u = jnp.concatenate([up_scr[i] for i in range(nblk)], axis=1).astype(F32)
    gate = _dot(perm, gate_ref[...])
    y = _gelu_tanh(y + d_ref[...] * u)
    z = _dot(y.astype(BF16), wglu_ref[...]) + bglu_ref[...]
    y = y * _sigmoid(z)
    out = (y * _silu(gate)).astype(BF16)
    o_ref[...] = _dot(permt_ref[...], out).astype(o_ref.dtype)


def _s5_tables(lam_re, lam_im, log_step, b_re, b_im, c_re, c_im, tl):
    groups = lam_re.shape[0]
    nblk = groups // S5_GROUPS_PER_BLOCK
    sub = tl // SUBLANES
    dt = jnp.exp(log_step.astype(F32))[:, None]
    lr, li = lam_re.astype(F32), lam_im.astype(F32)
    mag = jnp.exp(lr * dt)
    ar, ai = mag * jnp.cos(li * dt), mag * jnp.sin(li * dt)
    den = lr * lr + li * li
    nr, ni = ar - 1.0, ai
    cr, ci = (nr * lr + ni * li) / den, (ni * lr - nr * li) / den
    br, bi = b_re.astype(F32), b_im.astype(F32)
    bbar_re = cr[..., None] * br - ci[..., None] * bi
    bbar_im = cr[..., None] * bi + ci[..., None] * br

    gb = S5_GROUPS_PER_BLOCK
    eye = jnp.eye(gb, dtype=F32)

    def expand_blockdiag(w):
        w = w.reshape(nblk, gb, S5_STATE, S5_GROUP)
        return jnp.einsum('ngph,gk->nghkp', w, eye).reshape(nblk, gb * S5_GROUP, gb * S5_STATE)

    bbd = jnp.concatenate([expand_blockdiag(bbar_re), expand_blockdiag(bbar_im)], axis=2).astype(BF16)

    def contract_blockdiag(w):
        w = w.reshape(nblk, gb, S5_GROUP, S5_STATE)
        return jnp.einsum('nghp,gk->ngpkh', w, eye).reshape(nblk, gb * S5_STATE, gb * S5_GROUP)

    cbd = jnp.concatenate([contract_blockdiag(c_re.astype(F32)),
                           -contract_blockdiag(c_im.astype(F32))], axis=1).astype(BF16)

    def lam_pow(k):
        kk = jnp.asarray(k, F32)
        m = jnp.exp(lr * dt * kk)
        return ((m * jnp.cos(li * dt * kk)).reshape(nblk, gb * S5_STATE),
                (m * jnp.sin(li * dt * kk)).reshape(nblk, gb * S5_STATE))

    tab = []
    for k in (1, sub, 2 * sub, 4 * sub):
        pr, pi = lam_pow(k)
        tab += [pr, pi]
    ltab = jnp.stack(tab, axis=1)
    ltab = jnp.broadcast_to(ltab[:, :, None, :], (nblk, 8, SUBLANES, gb * S5_STATE))

    i = jnp.arange(tl)
    src = (i % SUBLANES) * sub + i // SUBLANES
    perm = (src[:, None] == jnp.arange(tl)[None, :]).astype(BF16)
    return bbd, cbd, ltab, perm, perm.T


def _s5_mixer(u_blk, gate, tables, d, w_glu, b_glu, bsz, length, tl):
    bbd, cbd, ltab, perm, permt = tables
    nblk = u_blk.shape[0]
    width = nblk * S5_BLOCK_IN
    nt = length // tl
    kern = functools.partial(_s5_kernel, tl=tl, nblk=nblk)
    const = lambda *shape: pl.BlockSpec(shape, lambda b, t: (0,) * len(shape))
    return pl.pallas_call(
        kern,
        grid=(bsz, nt),
        in_specs=[pl.BlockSpec((nblk, tl, S5_BLOCK_IN), lambda b, t: (0, b * nt + t, 0)),
                  pl.BlockSpec((tl, width), lambda b, t: (b * nt + t, 0)),
                  const(tl, tl), const(tl, tl),
                  const(*bbd.shape), const(*cbd.shape), const(*ltab.shape),
                  const(1, width), const(width, width), const(1, width)],
        out_specs=pl.BlockSpec((tl, width), lambda b, t: (b * nt + t, 0)),
        out_shape=jax.ShapeDtypeStruct((bsz * length, width), BF16),
        scratch_shapes=[pltpu.VMEM((S5_PAIRS, tl, LANES), F32),
                        pltpu.VMEM((S5_PAIRS, tl, LANES), F32),
                        pltpu.VMEM((nblk, 2, SUBLANES, S5_BLOCK_HALF), F32),
                        pltpu.VMEM((nblk, tl, S5_BLOCK_IN), F32),
                        pltpu.VMEM((nblk, tl, S5_BLOCK_IN), BF16)],
        compiler_params=_cparams(("parallel", "arbitrary")),
        name="s5_mixer",
    )(u_blk, gate, perm, permt, bbd, cbd, ltab, d.reshape(1, width).astype(F32),
      w_glu.astype(BF16), b_glu.reshape(1, width).astype(F32))


def _gdn_kernel(qkv_ref, ab_ref, gate_ref, convw_ref, gpar_ref, normw_ref, o_ref,
                ext_scr, q_scr, k_scr, v_scr, g_scr, beta_scr, state_scr, *, tl, qk_heads, v_heads):
    hd = GDN_HEAD_DIM
    qkw = qk_heads * hd
    rep = v_heads // qk_heads
    t_idx = pl.program_id(1)

    @pl.when(t_idx == 0)
    def _():
        state_scr[...] = jnp.zeros_like(state_scr)
        ext_scr[0:SUBLANES, :] = jnp.zeros((SUBLANES, ext_scr.shape[1]), F32)

    xp = qkv_ref[...].astype(F32)
    ext_scr[SUBLANES:SUBLANES + tl, :] = xp
    conv = convw_ref[0:1, :] * ext_scr[pl.ds(SUBLANES - 3, tl), :]
    for j in range(1, CONV_WIDTH):
        conv = conv + convw_ref[j:j + 1, :] * ext_scr[pl.ds(SUBLANES - 3 + j, tl), :]
    ext_scr[0:SUBLANES, :] = xp[tl - SUBLANES:tl, :]
    xc = _silu(conv)

    for h in range(qk_heads):
        qh = xc[:, h * hd:(h + 1) * hd]
        kh = xc[:, qkw + h * hd:qkw + (h + 1) * hd]
        qn = qh * lax.rsqrt(jnp.sum(qh * qh, axis=-1, keepdims=True) + NORM_EPS) * (hd ** -0.5)
        kn = kh * lax.rsqrt(jnp.sum(kh * kh, axis=-1, keepdims=True) + NORM_EPS)
        q_scr[:, h * hd:(h + 1) * hd] = qn
        k_scr[:, h * hd:(h + 1) * hd] = kn
    v_scr[...] = xc[:, 2 * qkw:]

    ab = ab_ref[...]
    xg = ab + gpar_ref[1:2, :]
    softplus = jnp.maximum(xg, 0.0) + jnp.log(1.0 + jnp.exp(-jnp.abs(xg)))
    g_scr[...] = gpar_ref[0:1, :] * softplus
    beta_scr[...] = _sigmoid(ab)

    ci = lax.broadcasted_iota(jnp.int32, (CHUNK, CHUNK), 0)
    cj = lax.broadcasted_iota(jnp.int32, (CHUNK, CHUNK), 1)
    causal = ci >= cj
    strict = ci > cj
    tril_f = causal.astype(F32)
    eye = (ci == cj).astype(F32)
    level_masks = []
    for b in (1, 2, 4, 8, 16, 32):
        level_masks.append(((ci // (2 * b)) == (cj // (2 * b))) & ((ci % (2 * b)) >= b) & ((cj % (2 * b)) < b))
    heads_q = range(qk_heads)
    heads_v = range(v_heads)

    def chunk_body(c, _):
        r0 = pl.multiple_of(c * CHUNK, CHUNK)
        rs = pl.ds(r0, CHUNK)
        gc = _dot_f32(tril_f, g_scr[rs, :])
        gct = gc.T
        beta_t = beta_scr[rs, :]
        eg = jnp.exp(gc)
        kh = [k_scr[rs, h * hd:(h + 1) * hd] for h in heads_q]
        qh = [q_scr[rs, h * hd:(h + 1) * hd] for h in heads_q]
        kq = [_dot_nt(jnp.concatenate([kh[h], qh[h]], axis=0).astype(BF16), kh[h].astype(BF16))
              for h in heads_q]
        g_col = [gc[:, h:h + 1] for h in heads_v]
        beta = [beta_t[:, v_heads + h:v_heads + h + 1] for h in heads_v]
        eg_col = [eg[:, h:h + 1] for h in heads_v]
        decay = [jnp.exp(jnp.where(causal, g_col[h] - gct[h:h + 1, :], -jnp.inf)) for h in heads_v]
        lmat = [jnp.where(strict, beta[h] * kq[h // rep][:CHUNK] * decay[h], 0.0) for h in heads_v]
        intra = [jnp.where(causal, kq[h // rep][CHUNK:] * decay[h], 0.0).astype(BF16) for h in heads_v]
        x = [eye - jnp.where(level_masks[0], lmat[h], 0.0) for h in heads_v]
        for m in level_masks[1:]:
            xb = [x[h].astype(BF16) for h in heads_v]
            t1 = [_dot(xb[h], jnp.where(m, lmat[h], 0.0).astype(BF16)) for h in heads_v]
            t2 = [_dot(t1[h].astype(BF16), xb[h]) for h in heads_v]
            x = [x[h] - t2[h] for h in heads_v]
        vh = [v_scr[rs, h * hd:(h + 1) * hd] for h in heads_v]
        uw = [_dot(x[h].astype(BF16),
                   jnp.concatenate([vh[h] * beta[h], kh[h // rep] * (beta[h] * eg_col[h])], axis=1).astype(BF16))
              for h in heads_v]
        s = [state_scr[h] for h in heads_v]
        ws = [_dot(jnp.concatenate([uw[h][:, hd:], qh[h // rep] * eg_col[h]], axis=0).astype(BF16),
                   s[h].astype(BF16)) for h in heads_v]
        v_new = [(uw[h][:, :hd] - ws[h][:CHUNK]).astype(BF16) for h in heads_v]
        o = [ws[h][CHUNK:] + _dot(intra[h], v_new[h]) for h in heads_v]
        g_last = [gc[CHUNK - 1:CHUNK, h:h + 1] for h in heads_v]
        for h in heads_v:
            k_dec = kh[h // rep] * jnp.exp(g_last[h] - g_col[h])
            state_scr[h] = s[h] * jnp.exp(g_last[h]) + _dot_tn(k_dec.astype(BF16), v_new[h])
        for h in heads_v:
            on = o[h] * lax.rsqrt(jnp.mean(o[h] * o[h], axis=-1, keepdims=True) + NORM_EPS) * normw_ref[...]
            gt = gate_ref[rs, h * hd:(h + 1) * hd].astype(F32)
            o_ref[rs, h * hd:(h + 1) * hd] = (on * _silu(gt)).astype(o_ref.dtype)
        return 0

    lax.fori_loop(0, tl // CHUNK, chunk_body, 0)


def _gdn_mixer(qkv, ab, gate, conv_w, a_log, dt_bias, norm_w, bsz, length, tl):
    v_heads = a_log.shape[0]
    qk_heads = v_heads // 2
    cw = qkv.shape[1]
    vw = v_heads * GDN_HEAD_DIM
    qkw = qk_heads * GDN_HEAD_DIM
    nt = length // tl
    gpar = jnp.zeros((2, LANES), F32)
    gpar = gpar.at[0, :v_heads].set(-jnp.exp(a_log.astype(F32)))
    gpar = gpar.at[1, :v_heads].set(dt_bias.astype(F32))
    kern = functools.partial(_gdn_kernel, tl=tl, qk_heads=qk_heads, v_heads=v_heads)
    const = lambda *shape: pl.BlockSpec(shape, lambda b, t: (0,) * len(shape))
    return pl.pallas_call(
        kern,
        grid=(bsz, nt),
        in_specs=[pl.BlockSpec((tl, cw), lambda b, t: (b * nt + t, 0)),
                  pl.BlockSpec((tl, LANES), lambda b, t: (b * nt + t, 0)),
                  pl.BlockSpec((tl, vw), lambda b, t: (b * nt + t, 0)),
                  const(CONV_WIDTH, cw), const(2, LANES), const(1, GDN_HEAD_DIM)],
        out_specs=pl.BlockSpec((tl, vw), lambda b, t: (b * nt + t, 0)),
        out_shape=jax.ShapeDtypeStruct((bsz * length, vw), BF16),
        scratch_shapes=[pltpu.VMEM((tl + SUBLANES, cw), F32),
                        pltpu.VMEM((tl, qkw), F32),
                        pltpu.VMEM((tl, qkw), F32),
                        pltpu.VMEM((tl, vw), F32),
                        pltpu.VMEM((tl, LANES), F32),
                        pltpu.VMEM((tl, LANES), F32),
                        pltpu.VMEM((v_heads, GDN_HEAD_DIM, GDN_HEAD_DIM), F32)],
        compiler_params=_cparams(("parallel", "arbitrary")),
        name="gdn_mixer",
    )(qkv, ab, gate, conv_w.astype(F32), gpar, norm_w.reshape(1, GDN_HEAD_DIM).astype(F32))


def _attn_out_kernel(ymix_ref, qx_ref, gx_ref, k_ref, v_ref, wa_ref, wb_ref, x_ref, fw_ref, o_ref,
                     *, final_norm):
    xw = qx_ref.shape[1]
    hd = xw // XATTN_HEADS
    q = qx_ref[...]
    ys = []
    for h in range(XATTN_HEADS):
        hs = slice(h * hd, (h + 1) * hd)
        s = _dot_nt(q[:, hs], k_ref[:, hs]) * (hd ** -0.5)
        s = s - jnp.max(s, axis=-1, keepdims=True)
        p = jnp.exp(s)
        p = p / jnp.sum(p, axis=-1, keepdims=True)
        ys.append(_dot(p.astype(BF16), v_ref[:, hs]))
    yx = jnp.concatenate(ys, axis=1) * _silu(gx_ref[...].astype(F32))
    acc = _dot(ymix_ref[...], wa_ref[...]) + _dot(yx.astype(BF16), wb_ref[...])
    xn = x_ref[...] + acc
    if final_norm:
        xn = xn * lax.rsqrt(jnp.mean(xn * xn, axis=-1, keepdims=True) + NORM_EPS) * fw_ref[...]
    o_ref[...] = xn


def _attn_out(ymix, qx, gx, k, v, w_out, x2d, final_w, bsz, length, tm, final_norm):
    t, d = x2d.shape
    mw = ymix.shape[1]
    xw = qx.shape[1]
    mem = k.shape[0] // bsz
    nt = length // tm
    wa = w_out[:mw].astype(BF16)
    wb = w_out[mw:].astype(BF16)
    kern = functools.partial(_attn_out_kernel, final_norm=final_norm)
    row = lambda width: pl.BlockSpec((tm, width), lambda b, i: (b * nt + i, 0))
    const = lambda *shape: pl.BlockSpec(shape, lambda b, i: (0,) * len(shape))
    return pl.pallas_call(
        kern,
        grid=(bsz, nt),
        in_specs=[row(mw), row(xw), row(xw),
                  pl.BlockSpec((mem, xw), lambda b, i: (b, 0)),
                  pl.BlockSpec((mem, xw), lambda b, i: (b, 0)),
                  const(mw, d), const(xw, d), row(d), const(1, d)],
        out_specs=row(d),
        out_shape=jax.ShapeDtypeStruct((t, d), F32),
        compiler_params=_cparams(("parallel", "parallel")),
        name="attn_out",
    )(ymix, qx, gx, k, v, wa, wb, x2d, final_w.reshape(1, d).astype(F32))


def _pick_tile(n, pref):
    t = pref
    while n % t:
        t //= 2
    return t


def kernel(x, mem, norm_w, w_out, mem_norm_w, w_mem_kv, s5_w_in, s5_lambda_re, s5_lambda_im, s5_log_step, s5_b_re, s5_b_im, s5_c_re, s5_c_im, s5_d, s5_w_glu, s5_b_glu, gdn_w_in, gdn_conv_w, gdn_a_log, gdn_dt_bias, gdn_norm_w, final_norm_w):
    bsz, length, d = x.shape
    mem_tokens = mem.shape[1]
    depth = norm_w.shape[0]
    mix_width = w_out.shape[1]
    xw = mix_width // 4
    tw = mix_width - xw
    v_heads = gdn_a_log.shape[1]
    qkw = (v_heads // 2) * GDN_HEAD_DIM

    t = bsz * length
    x2d = x.reshape(t, d)
    mem2d = mem.reshape(bsz * mem_tokens, d)
    tm_proj = _pick_tile(t, 512)
    tm_mem = _pick_tile(bsz * mem_tokens, 256)
    tl_s5 = _pick_tile(length, 256)
    tl_gdn = _pick_tile(length, 256)
    tm_out = _pick_tile(length, 512)

    for i in range(depth):
        j = i // 2
        kmem, vmem = _norm_proj(mem2d, mem_norm_w[i], w_mem_kv[i].astype(BF16),
                                [(0, xw, BF16, 0), (xw, 2 * xw, BF16, 0)], tm_mem)
        if i % 2 == 0:
            u_blk, gate_mix, qx, gx = _norm_proj(
                x2d, norm_w[i], s5_w_in[j].astype(BF16),
                [(0, tw, BF16, S5_BLOCK_IN), (tw, 2 * tw, BF16, 0),
                 (2 * tw, 2 * tw + xw, BF16, 0), (2 * tw + xw, 2 * tw + 2 * xw, BF16, 0)], tm_proj)
            tables = _s5_tables(s5_lambda_re[j], s5_lambda_im[j], s5_log_step[j], s5_b_re[j], s5_b_im[j],
                                s5_c_re[j], s5_c_im[j], tl_s5)
            ymix = _s5_mixer(u_blk, gate_mix, tables, s5_d[j], s5_w_glu[j], s5_b_glu[j], bsz, length, tl_s5)
        else:
            w = gdn_w_in[j]
            o1 = 2 * qkw + tw
            o3 = o1 + 2 * v_heads
            o4 = o3 + tw
            ab_w = jnp.zeros((d, LANES), w.dtype).at[:, :2 * v_heads].set(w[:, o1:o3])
            wp = jnp.concatenate([w[:, :o1], w[:, o3:], ab_w], axis=1).astype(BF16)
            c_gate = o1
            c_qx = c_gate + tw
            c_gx = c_qx + xw
            c_ab = c_gx + xw
            qkv, gate_mix, qx, gx, ab = _norm_proj(
                x2d, norm_w[i], wp,
                [(0, o1, BF16, 0), (c_gate, c_qx, BF16, 0), (c_qx, c_gx, BF16, 0),
                 (c_gx, c_ab, BF16, 0), (c_ab, c_ab + LANES, F32, 0)], tm_proj)
            ymix = _gdn_mixer(qkv, ab, gate_mix, gdn_conv_w[j], gdn_a_log[j], gdn_dt_bias[j],
                              gdn_norm_w[j], bsz, length, tl_gdn)
        x2d = _attn_out(ymix, qx, gx, kmem, vmem, w_out[i], x2d, final_norm_w, bsz, length, tm_out,
                        final_norm=(i == depth - 1))
    return x2d.reshape(bsz, length, d)
```

```python
import functools
import math

import jax
import jax.numpy as jnp
from jax import lax
from jax.experimental import pallas as pl
from jax.experimental.pallas import tpu as pltpu

F32 = jnp.float32
BF16 = jnp.bfloat16

NORM_EPS = 1e-6
XATTN_HEADS = 4
S5_GROUP = 16
S5_STATE = 64
GDN_HEAD_DIM = 128
CONV_WIDTH = 4
CHUNK = 64

LANES = 128
SUBLANES = 8
VMEM_LIMIT = 56 * 1024 * 1024

S5_GROUPS_PER_BLOCK = 16
S5_BLOCK_IN = S5_GROUPS_PER_BLOCK * S5_GROUP
S5_BLOCK_HALF = S5_GROUPS_PER_BLOCK * S5_STATE
S5_PAIRS = S5_BLOCK_HALF // LANES


def _dot(a, b):
    return jnp.dot(a, b, preferred_element_type=F32)


def _dot_nt(a, b):
    return lax.dot_general(a, b, (((1,), (1,)), ((), ())), preferred_element_type=F32)


def _dot_tn(a, b):
    return lax.dot_general(a, b, (((0,), (0,)), ((), ())), preferred_element_type=F32)


def _dot_f32(a, b):
    return jnp.dot(a, b, preferred_element_type=F32, precision=lax.Precision.HIGHEST)


def _sigmoid(x):
    return 1.0 / (1.0 + jnp.exp(-x))


def _silu(x):
    return x * _sigmoid(x)


def _gelu_tanh(x):
    c = math.sqrt(2.0 / math.pi)
    return 0.5 * x * (1.0 + jnp.tanh(c * (x + 0.044715 * (x * x * x))))


def _cparams(sem):
    return pltpu.CompilerParams(dimension_semantics=sem, vmem_limit_bytes=VMEM_LIMIT)


def _norm_proj_kernel(x_ref, nw_ref, w_ref, *o_refs, col_splits, blocked):
    x = x_ref[...]
    h = x * lax.rsqrt(jnp.mean(x * x, axis=-1, keepdims=True) + NORM_EPS)
    hb = (h * nw_ref[...]).astype(BF16)
    for o_ref, (c0, c1), blk in zip(o_refs, col_splits, blocked):
        if blk:
            for i in range((c1 - c0) // blk):
                o_ref[i] = _dot(hb, w_ref[:, c0 + i * blk:c0 + (i + 1) * blk]).astype(o_ref.dtype)
        else:
            step = 512
            for a in range(c0, c1, step):
                b = min(a + step, c1)
                o_ref[:, a - c0:b - c0] = _dot(hb, w_ref[:, a:b]).astype(o_ref.dtype)


def _norm_proj(x2d, nw, w_bf16, outs, tm):
    t, d = x2d.shape
    n = w_bf16.shape[1]
    out_shapes, out_specs = [], []
    for c0, c1, dt, blk in outs:
        if blk:
            nb = (c1 - c0) // blk
            out_shapes.append(jax.ShapeDtypeStruct((nb, t, blk), dt))
            out_specs.append(pl.BlockSpec((nb, tm, blk), lambda i: (0, i, 0)))
        else:
            out_shapes.append(jax.ShapeDtypeStruct((t, c1 - c0), dt))
            out_specs.append(pl.BlockSpec((tm, c1 - c0), lambda i: (i, 0)))
    kern = functools.partial(_norm_proj_kernel,
                             col_splits=tuple((o[0], o[1]) for o in outs),
                             blocked=tuple(o[3] for o in outs))
    return pl.pallas_call(
        kern,
        grid=(t // tm,),
        in_specs=[pl.BlockSpec((tm, d), lambda i: (i, 0)),
                  pl.BlockSpec((1, d), lambda i: (0, 0)),
                  pl.BlockSpec((d, n), lambda i: (0, 0))],
        out_specs=out_specs,
        out_shape=out_shapes,
        compiler_params=_cparams(("parallel",)),
        name="norm_proj",
    )(x2d, nw.reshape(1, d), w_bf16)


def _cmul(ar, ai, xr, xi):
    return ar * xr - ai * xi, ar * xi + ai * xr


def _shift_rows(x, k, rows):
    return jnp.where(rows >= k, pltpu.roll(x, k, axis=0), 0.0)


def _s5_kernel(u_ref, gate_ref, perm_ref, permt_ref, bbd_ref, cbd_ref, ltab_ref, d_ref, wglu_ref, bglu_ref,
               o_ref, xre_scr, xim_scr, carry_scr, y_scr, up_scr, *, tl, nblk):
    sub = tl // SUBLANES
    t_idx = pl.program_id(1)

    @pl.when(t_idx == 0)
    def _():
        carry_scr[...] = jnp.zeros_like(carry_scr)

    rows = lax.broadcasted_iota(jnp.int32, (SUBLANES, LANES), 0)
    perm = perm_ref[...]

    def expand(cb):
        up = _dot(perm, u_ref[cb]).astype(BF16)
        up_scr[cb] = up
        bu = _dot(up, bbd_ref[cb])
        for n in range(S5_PAIRS):
            xre_scr[cb % 2, n] = bu[:, n * LANES:(n + 1) * LANES]
            xim_scr[cb % 2, n] = bu[:, S5_BLOCK_HALF + n * LANES:S5_BLOCK_HALF + (n + 1) * LANES]

    def scan_and_contract(cb):
        xre_b, xim_b = xre_scr.at[cb % 2], xim_scr.at[cb % 2]
        for grp in range(0, S5_PAIRS, 4):
            pairs = list(range(grp, grp + 4))
            lam = [(ltab_ref[cb, 0, :, n * LANES:(n + 1) * LANES],
                    ltab_ref[cb, 1, :, n * LANES:(n + 1) * LANES]) for n in pairs]
            st = [(jnp.zeros((SUBLANES, LANES), F32), jnp.zeros((SUBLANES, LANES), F32)) for _ in pairs]
            for j in range(sub):
                for q, n in enumerate(pairs):
                    ar, ai = lam[q]
                    xr, xi = st[q]
                    pr, pi = _cmul(ar, ai, xr, xi)
                    st[q] = (pr + xre_b[n, pl.ds(j * SUBLANES, SUBLANES), :],
                             pi + xim_b[n, pl.ds(j * SUBLANES, SUBLANES), :])
            for q, n in enumerate(pairs):
                ls = slice(n * LANES, (n + 1) * LANES)
                er, ei = st[q]
                fr = pltpu.roll(jnp.where(rows == SUBLANES - 1, carry_scr[cb, 0, :, ls], er), 1, axis=0)
                fi = pltpu.roll(jnp.where(rows == SUBLANES - 1, carry_scr[cb, 1, :, ls], ei), 1, axis=0)
                for lvl, k in enumerate((1, 2, 4)):
                    pr_, pi_ = ltab_ref[cb, 2 + 2 * lvl, :, ls], ltab_ref[cb, 3 + 2 * lvl, :, ls]
                    sr, si = _shift_rows(fr, k, rows), _shift_rows(fi, k, rows)
                    mr, mi = _cmul(pr_, pi_, sr, si)
                    fr, fi = fr + mr, fi + mi
                st[q] = (fr, fi)
            for j in range(sub):
                for q, n in enumerate(pairs):
                    ar, ai = lam[q]
                    xr, xi = st[q]
                    pr, pi = _cmul(ar, ai, xr, xi)
                    xr = pr + xre_b[n, pl.ds(j * SUBLANES, SUBLANES), :]
                    xi = pi + xim_b[n, pl.ds(j * SUBLANES, SUBLANES), :]
                    xre_b[n, pl.ds(j * SUBLANES, SUBLANES), :] = xr
                    xim_b[n, pl.ds(j * SUBLANES, SUBLANES), :] = xi
                    st[q] = (xr, xi)
            for q, n in enumerate(pairs):
                ls = slice(n * LANES, (n + 1) * LANES)
                carry_scr[cb, 0, :, ls] = st[q][0]
                carry_scr[cb, 1, :, ls] = st[q][1]

        xs = jnp.concatenate([xre_b[n] for n in range(S5_PAIRS)]
                             + [xim_b[n] for n in range(S5_PAIRS)], axis=1).astype(BF16)
        y_scr[cb] = _dot(xs, cbd_ref[cb])

    expand(0)
    for cb in range(nblk):
        if cb + 1 < nblk:
            expand(cb + 1)
        scan_and_contract(cb)

    y = jnp.concatenate([y_scr[i] for i in range(nblk)], axis=1)
    u = jnp.concatenate([up_scr[i] for i in range(nblk)], axis=1).astype(F32)
    gate = _dot(perm, gate_ref[...])
    y = _gelu_tanh(y + d_ref[...] * u)
    z = _dot(y.astype(BF16), wglu_ref[...]) + bglu_ref[...]
    y = y * _sigmoid(z)
    out = (y * _silu(gate)).astype(BF16)
    o_ref[...] = _dot(permt_ref[...], out).astype(o_ref.dtype)


def _s5_tables(lam_re, lam_im, log_step, b_re, b_im, c_re, c_im, tl):
    groups = lam_re.shape[0]
    nblk = groups // S5_GROUPS_PER_BLOCK
    sub = tl // SUBLANES
    dt = jnp.exp(log_step.astype(F32))[:, None]
    lr, li = lam_re.astype(F32), lam_im.astype(F32)
    mag = jnp.exp(lr * dt)
    ar, ai = mag * jnp.cos(li * dt), mag * jnp.sin(li * dt)
    den = lr * lr + li * li
    nr, ni = ar - 1.0, ai
    cr, ci = (nr * lr + ni * li) / den, (ni * lr - nr * li) / den
    br, bi = b_re.astype(F32), b_im.astype(F32)
    bbar_re = cr[..., None] * br - ci[..., None] * bi
    bbar_im = cr[..., None] * bi + ci[..., None] * br

    gb = S5_GROUPS_PER_BLOCK
    eye = jnp.eye(gb, dtype=F32)

    def expand_blockdiag(w):
        w = w.reshape(nblk, gb, S5_STATE, S5_GROUP)
        return jnp.einsum('ngph,gk->nghkp', w, eye).reshape(nblk, gb * S5_GROUP, gb * S5_STATE)

    bbd = jnp.concatenate([expand_blockdiag(bbar_re), expand_blockdiag(bbar_im)], axis=2).astype(BF16)

    def contract_blockdiag(w):
        w = w.reshape(nblk, gb, S5_GROUP, S5_STATE)
        return jnp.einsum('nghp,gk->ngpkh', w, eye).reshape(nblk, gb * S5_STATE, gb * S5_GROUP)

    cbd = jnp.concatenate([contract_blockdiag(c_re.astype(F32)),
                           -contract_blockdiag(c_im.astype(F32))], axis=1).astype(BF16)

    def lam_pow(k):
        kk = jnp.asarray(k, F32)
        m = jnp.exp(lr * dt * kk)
        return ((m * jnp.cos(li * dt * kk)).reshape(nblk, gb * S5_STATE),
                (m * jnp.sin(li * dt * kk)).reshape(nblk, gb * S5_STATE))

    tab = []
    for k in (1, sub, 2 * sub, 4 * sub):
        pr, pi = lam_pow(k)
        tab += [pr, pi]
    ltab = jnp.stack(tab, axis=1)
    ltab = jnp.broadcast_to(ltab[:, :, None, :], (nblk, 8, SUBLANES, gb * S5_STATE))

    i = jnp.arange(tl)
    src = (i % SUBLANES) * sub + i // SUBLANES
    perm = (src[:, None] == jnp.arange(tl)[None, :]).astype(BF16)
    return bbd, cbd, ltab, perm, perm.T


def _s5_mixer(u_blk, gate, tables, d, w_glu, b_glu, bsz, length, tl):
    bbd, cbd, ltab, perm, permt = tables
    nblk = u_blk.shape[0]
    width = nblk * S5_BLOCK_IN
    nt = length // tl
    kern = functools.partial(_s5_kernel, tl=tl, nblk=nblk)
    const = lambda *shape: pl.BlockSpec(shape, lambda b, t: (0,) * len(shape))
    return pl.pallas_call(
        kern,
        grid=(bsz, nt),
        in_specs=[pl.BlockSpec((nblk, tl, S5_BLOCK_IN), lambda b, t: (0, b * nt + t, 0)),
                  pl.BlockSpec((tl, width), lambda b, t: (b * nt + t, 0)),
                  const(tl, tl), const(tl, tl),
                  const(*bbd.shape), const(*cbd.shape), const(*ltab.shape),
                  const(1, width), const(width, width), const(1, width)],
        out_specs=pl.BlockSpec((tl, width), lambda b, t: (b * nt + t, 0)),
        out_shape=jax.ShapeDtypeStruct((bsz * length, width), BF16),
        scratch_shapes=[pltpu.VMEM((2, S5_PAIRS, tl, LANES), F32),
                        pltpu.VMEM((2, S5_PAIRS, tl, LANES), F32),
                        pltpu.VMEM((nblk, 2, SUBLANES, S5_BLOCK_HALF), F32),
                        pltpu.VMEM((nblk, tl, S5_BLOCK_IN), F32),
                        pltpu.VMEM((nblk, tl, S5_BLOCK_IN), BF16)],
        compiler_params=_cparams(("parallel", "arbitrary")),
        name="s5_mixer",
    )(u_blk, gate, perm, permt, bbd, cbd, ltab, d.reshape(1, width).astype(F32),
      w_glu.astype(BF16), b_glu.reshape(1, width).astype(F32))


def _gdn_kernel(qkv_ref, ab_ref, gate_ref, convw_ref, gpar_ref, normw_ref, o_ref,
                ext_scr, g_scr, beta_scr, state_scr, *, tl, qk_heads, v_heads):
    hd = GDN_HEAD_DIM
    qkw = qk_heads * hd
    rep = v_heads // qk_heads
    t_idx = pl.program_id(1)

    @pl.when(t_idx == 0)
    def _():
        state_scr[...] = jnp.zeros_like(state_scr)
        ext_scr[0:SUBLANES, :] = jnp.zeros((SUBLANES, ext_scr.shape[1]), F32)

    ext_scr[SUBLANES:SUBLANES + tl, :] = qkv_ref[...].astype(F32)

    def conv_silu(r0, n):
        conv = convw_ref[0:1, :] * ext_scr[pl.ds(r0 + SUBLANES - 3, n), :]
        for j in range(1, CONV_WIDTH):
            conv = conv + convw_ref[j:j + 1, :] * ext_scr[pl.ds(r0 + SUBLANES - 3 + j, n), :]
        return _silu(conv)

    ab = ab_ref[...]
    xg = ab + gpar_ref[1:2, :]
    softplus = jnp.maximum(xg, 0.0) + jnp.log(1.0 + jnp.exp(-jnp.abs(xg)))
    g_scr[...] = gpar_ref[0:1, :] * softplus
    beta_scr[...] = _sigmoid(ab)

    pr = 2 * CHUNK
    assert hd == pr
    ci = lax.broadcasted_iota(jnp.int32, (pr, pr), 0)
    cj = lax.broadcasted_iota(jnp.int32, (pr, pr), 1)
    same = (ci // CHUNK) == (cj // CHUNK)
    causal = (ci >= cj) & same
    strict = (ci > cj) & same
    tril_f = causal.astype(F32)
    eye = (ci == cj).astype(F32)
    levels = (1, 2, 4, 8, 16, 32)
    level_masks = {b: ((ci // (2 * b)) == (cj // (2 * b))) & ((ci % (2 * b)) >= b) & ((cj % (2 * b)) < b)
                   for b in levels}
    heads_q = range(qk_heads)
    heads_v = range(v_heads)

    def pair_body(r0):
        rs = slice(r0, r0 + pr)
        xc = conv_silu(r0, pr)
        qn, kn = [], []
        for h in heads_q:
            qh = xc[:, h * hd:(h + 1) * hd]
            kh = xc[:, qkw + h * hd:qkw + (h + 1) * hd]
            qn.append(qh * lax.rsqrt(jnp.sum(qh * qh, axis=-1, keepdims=True) + NORM_EPS) * (hd ** -0.5))
            kn.append(kh * lax.rsqrt(jnp.sum(kh * kh, axis=-1, keepdims=True) + NORM_EPS))
        gc = _dot_f32(tril_f, g_scr[rs, :])
        gct = gc.T
        beta_t = beta_scr[rs, :]
        kq = [_dot_nt(jnp.concatenate([kn[h], qn[h]], axis=0).astype(BF16), kn[h].astype(BF16))
              for h in heads_q]
        g_b = [jnp.broadcast_to(gc[:, h:h + 1], (pr, pr)) for h in heads_v]
        beta_b = [jnp.broadcast_to(beta_t[:, v_heads + h:v_heads + h + 1], (pr, pr)) for h in heads_v]
        eg_b = [jnp.exp(g_b[h]) for h in heads_v]
        decay = [jnp.exp(jnp.where(causal, g_b[h] - gct[h:h + 1, :], -jnp.inf)) for h in heads_v]
        lmat = [jnp.where(strict, beta_b[h] * kq[h // rep][:pr] * decay[h], 0.0) for h in heads_v]
        intra = [jnp.where(causal, kq[h // rep][pr:] * decay[h], 0.0).astype(BF16) for h in heads_v]
        x = [eye - jnp.where(level_masks[1], lmat[h], 0.0) for h in heads_v]
        for b in levels[1:]:
            m = level_masks[b]
            xb = [x[h].astype(BF16) for h in heads_v]
            t1 = [_dot(xb[h], jnp.where(m, lmat[h], 0.0).astype(BF16)) for h in heads_v]
            t2 = [_dot(t1[h].astype(BF16), xb[h]) for h in heads_v]
            x = [x[h] - t2[h] for h in heads_v]
        uw = [_dot(x[h].astype(BF16),
                   jnp.concatenate([xc[:, 2 * qkw + h * hd:2 * qkw + (h + 1) * hd] * beta_b[h],
                                    kn[h // rep] * (beta_b[h] * eg_b[h])], axis=1).astype(BF16))
              for h in heads_v]
        qe = [qn[h // rep] * eg_b[h] for h in heads_v]
        s = [state_scr[h] for h in heads_v]
        o_state, v_new = [[] for _ in heads_v], [[] for _ in heads_v]
        for c in range(2):
            cs = slice(c * CHUNK, (c + 1) * CHUNK)
            last = c * CHUNK + CHUNK - 1
            ws = [_dot(jnp.concatenate([uw[h][cs, hd:], qe[h][cs]], axis=0).astype(BF16), s[h].astype(BF16))
                  for h in heads_v]
            for h in heads_v:
                v_new[h].append((uw[h][cs, :hd] - ws[h][:CHUNK]).astype(BF16))
                o_state[h].append(ws[h][CHUNK:])
            g_last = [gc[last:last + 1, h:h + 1] for h in heads_v]
            s = [s[h] * jnp.exp(g_last[h])
                 + _dot_tn((kn[h // rep][cs] * jnp.exp(g_last[h] - g_b[h][cs])).astype(BF16), v_new[h][c])
                 for h in heads_v]
        for h in heads_v:
            state_scr[h] = s[h]
        o = [jnp.concatenate(o_state[h], axis=0) + _dot(intra[h], jnp.concatenate(v_new[h], axis=0))
             for h in heads_v]
        for h in heads_v:
            on = o[h] * lax.rsqrt(jnp.mean(o[h] * o[h], axis=-1, keepdims=True) + NORM_EPS) * normw_ref[...]
            gt = gate_ref[rs, h * hd:(h + 1) * hd].astype(F32)
            o_ref[rs, h * hd:(h + 1) * hd] = (on * _silu(gt)).astype(o_ref.dtype)

    for p in range(tl // pr):
        pair_body(p * pr)
    ext_scr[0:SUBLANES, :] = ext_scr[tl:tl + SUBLANES, :]


def _gdn_mixer(qkv, ab, gate, conv_w, a_log, dt_bias, norm_w, bsz, length, tl):
    v_heads = a_log.shape[0]
    qk_heads = v_heads // 2
    cw = qkv.shape[1]
    vw = v_heads * GDN_HEAD_DIM
    qkw = qk_heads * GDN_HEAD_DIM
    nt = length // tl
    gpar = jnp.zeros((2, LANES), F32)
    gpar = gpar.at[0, :v_heads].set(-jnp.exp(a_log.astype(F32)))
    gpar = gpar.at[1, :v_heads].set(dt_bias.astype(F32))
    kern = functools.partial(_gdn_kernel, tl=tl, qk_heads=qk_heads, v_heads=v_heads)
    const = lambda *shape: pl.BlockSpec(shape, lambda b, t: (0,) * len(shape))
    return pl.pallas_call(
        kern,
        grid=(bsz, nt),
        in_specs=[pl.BlockSpec((tl, cw), lambda b, t: (b * nt + t, 0)),
                  pl.BlockSpec((tl, LANES), lambda b, t: (b * nt + t, 0)),
                  pl.BlockSpec((tl, vw), lambda b, t: (b * nt + t, 0)),
                  const(CONV_WIDTH, cw), const(2, LANES), const(1, GDN_HEAD_DIM)],
        out_specs=pl.BlockSpec((tl, vw), lambda b, t: (b * nt + t, 0)),
        out_shape=jax.ShapeDtypeStruct((bsz * length, vw), BF16),
        scratch_shapes=[pltpu.VMEM((tl + SUBLANES, cw), F32),
                        pltpu.VMEM((tl, LANES), F32),
                        pltpu.VMEM((tl, LANES), F32),
                        pltpu.VMEM((v_heads, GDN_HEAD_DIM, GDN_HEAD_DIM), F32)],
        compiler_params=_cparams(("parallel", "arbitrary")),
        name="gdn_mixer",
    )(qkv, ab, gate, conv_w.astype(F32), gpar, norm_w.reshape(1, GDN_HEAD_DIM).astype(F32))


def _attn_out_kernel(ymix_ref, qx_ref, gx_ref, k_ref, v_ref, wa_ref, wb_ref, x_ref, fw_ref, o_ref,
                     *, final_norm):
    xw = qx_ref.shape[1]
    hd = xw // XATTN_HEADS
    q = qx_ref[...]
    ys = []
    for h in range(XATTN_HEADS):
        hs = slice(h * hd, (h + 1) * hd)
        s = _dot_nt(q[:, hs], k_ref[:, hs]) * (hd ** -0.5)
        s = s - jnp.max(s, axis=-1, keepdims=True)
        p = jnp.exp(s)
        p = p / jnp.sum(p, axis=-1, keepdims=True)
        ys.append(_dot(p.astype(BF16), v_ref[:, hs]))
    yx = jnp.concatenate(ys, axis=1) * _silu(gx_ref[...].astype(F32))
    acc = _dot(ymix_ref[...], wa_ref[...]) + _dot(yx.astype(BF16), wb_ref[...])
    xn = x_ref[...] + acc
    if final_norm:
        xn = xn * lax.rsqrt(jnp.mean(xn * xn, axis=-1, keepdims=True) + NORM_EPS) * fw_ref[...]
    o_ref[...] = xn


def _attn_out(ymix, qx, gx, k, v, w_out, x2d, final_w, bsz, length, tm, final_norm):
    t, d = x2d.shape
    mw = ymix.shape[1]
    xw = qx.shape[1]
    mem = k.shape[0] // bsz
    nt = length // tm
    wa = w_out[:mw].astype(BF16)
    wb = w_out[mw:].astype(BF16)
    kern = functools.partial(_attn_out_kernel, final_norm=final_norm)
    row = lambda width: pl.BlockSpec((tm, width), lambda b, i: (b * nt + i, 0))
    const = lambda *shape: pl.BlockSpec(shape, lambda b, i: (0,) * len(shape))
    return pl.pallas_call(
        kern,
        grid=(bsz, nt),
        in_specs=[row(mw), row(xw), row(xw),
                  pl.BlockSpec((mem, xw), lambda b, i: (b, 0)),
                  pl.BlockSpec((mem, xw), lambda b, i: (b, 0)),
                  const(mw, d), const(xw, d), row(d), const(1, d)],
        out_specs=row(d),
        out_shape=jax.ShapeDtypeStruct((t, d), F32),
        compiler_params=_cparams(("parallel", "parallel")),
        name="attn_out",
    )(ymix, qx, gx, k, v, wa, wb, x2d, final_w.reshape(1, d).astype(F32))


def _pick_tile(n, pref):
    t = pref
    while n % t:
        t //= 2
    return t


def kernel(x, mem, norm_w, w_out, mem_norm_w, w_mem_kv, s5_w_in, s5_lambda_re, s5_lambda_im, s5_log_step, s5_b_re, s5_b_im, s5_c_re, s5_c_im, s5_d, s5_w_glu, s5_b_glu, gdn_w_in, gdn_conv_w, gdn_a_log, gdn_dt_bias, gdn_norm_w, final_norm_w):
    bsz, length, d = x.shape
    mem_tokens = mem.shape[1]
    depth = norm_w.shape[0]
    mix_width = w_out.shape[1]
    xw = mix_width // 4
    tw = mix_width - xw
    v_heads = gdn_a_log.shape[1]
    qkw = (v_heads // 2) * GDN_HEAD_DIM

    t = bsz * length
    x2d = x.reshape(t, d)
    mem2d = mem.reshape(bsz * mem_tokens, d)
    tm_proj = _pick_tile(t, 512)
    tm_mem = _pick_tile(bsz * mem_tokens, 256)
    tl_s5 = _pick_tile(length, 256)
    tl_gdn = _pick_tile(length, 256)
    tm_out = _pick_tile(length, 512)

    for i in range(depth):
        j = i // 2
        kmem, vmem = _norm_proj(mem2d, mem_norm_w[i], w_mem_kv[i].astype(BF16),
                                [(0, xw, BF16, 0), (xw, 2 * xw, BF16, 0)], tm_mem)
        if i % 2 == 0:
            u_blk, gate_mix, qx, gx = _norm_proj(
                x2d, norm_w[i], s5_w_in[j].astype(BF16),
                [(0, tw, BF16, S5_BLOCK_IN), (tw, 2 * tw, BF16, 0),
                 (2 * tw, 2 * tw + xw, BF16, 0), (2 * tw + xw, 2 * tw + 2 * xw, BF16, 0)], tm_proj)
            tables = _s5_tables(s5_lambda_re[j], s5_lambda_im[j], s5_log_step[j], s5_b_re[j], s5_b_im[j],
                                s5_c_re[j], s5_c_im[j], tl_s5)
            ymix = _s5_mixer(u_blk, gate_mix, tables, s5_d[j], s5_w_glu[j], s5_b_glu[j], bsz, length, tl_s5)
        else:
            w = gdn_w_in[j]
            o1 = 2 * qkw + tw
            o3 = o1 + 2 * v_heads
            o4 = o3 + tw
            ab_w = jnp.zeros((d, LANES), w.dtype).at[:, :2 * v_heads].set(w[:, o1:o3])
            wp = jnp.concatenate([w[:, :o1], w[:, o3:], ab_w], axis=1).astype(BF16)
            c_gate = o1
            c_qx = c_gate + tw
            c_gx = c_qx + xw
            c_ab = c_gx + xw
            qkv, gate_mix, qx, gx, ab = _norm_proj(
                x2d, norm_w[i], wp,
                [(0, o1, BF16, 0), (c_gate, c_qx, BF16, 0), (c_qx, c_gx, BF16, 0),
                 (c_gx, c_ab, BF16, 0), (c_ab, c_ab + LANES, F32, 0)], tm_proj)
            ymix = _gdn_mixer(qkv, ab, gate_mix, gdn_conv_w[j], gdn_a_log[j], gdn_dt_bias[j],
                              gdn_norm_w[j], bsz, length, tl_gdn)
        x2d = _attn_out(ymix, qx, gx, kmem, vmem, w_out[i], x2d, final_norm_w, bsz, length, tm_out,
                        final_norm=(i == depth - 1))
    return x2d.reshape(bsz, length, d)
```

```python
import functools
import math

import jax
import jax.numpy as jnp
from jax import lax
from jax.experimental import pallas as pl
from jax.experimental.pallas import tpu as pltpu

F32 = jnp.float32
BF16 = jnp.bfloat16

NORM_EPS = 1e-6
XATTN_HEADS = 4
S5_GROUP = 16
S5_STATE = 64
GDN_HEAD_DIM = 128
CONV_WIDTH = 4
CHUNK = 64
GDN_HIST = 2 * CHUNK

LANES = 128
SUBLANES = 8
VMEM_LIMIT = 56 * 1024 * 1024

S5_GROUPS_PER_BLOCK = 16
S5_BLOCK_IN = S5_GROUPS_PER_BLOCK * S5_GROUP
S5_BLOCK_HALF = S5_GROUPS_PER_BLOCK * S5_STATE
S5_PAIRS = S5_BLOCK_HALF // LANES
S5_SLOTS = 3


def _dot(a, b):
    return jnp.dot(a, b, preferred_element_type=F32)


def _dot_nt(a, b):
    return lax.dot_general(a, b, (((1,), (1,)), ((), ())), preferred_element_type=F32)


def _dot_tn(a, b):
    return lax.dot_general(a, b, (((0,), (0,)), ((), ())), preferred_element_type=F32)


def _dot_f32(a, b):
    return jnp.dot(a, b, preferred_element_type=F32, precision=lax.Precision.HIGHEST)


def _sigmoid(x):
    return 1.0 / (1.0 + jnp.exp(-x))


def _silu(x):
    return x * _sigmoid(x)


def _gelu_tanh(x):
    c = math.sqrt(2.0 / math.pi)
    return 0.5 * x * (1.0 + jnp.tanh(c * (x + 0.044715 * (x * x * x))))


def _cparams(sem):
    return pltpu.CompilerParams(dimension_semantics=sem, vmem_limit_bytes=VMEM_LIMIT)


def _norm_proj_kernel(x_ref, nw_ref, w_ref, *o_refs, col_splits, blocked):
    x = x_ref[...]
    h = x * lax.rsqrt(jnp.mean(x * x, axis=-1, keepdims=True) + NORM_EPS)
    hb = (h * nw_ref[...]).astype(BF16)
    for o_ref, (c0, c1), blk in zip(o_refs, col_splits, blocked):
        if blk:
            for i in range((c1 - c0) // blk):
                o_ref[i] = _dot(hb, w_ref[:, c0 + i * blk:c0 + (i + 1) * blk]).astype(o_ref.dtype)
        else:
            step = 512
            for a in range(c0, c1, step):
                b = min(a + step, c1)
                o_ref[:, a - c0:b - c0] = _dot(hb, w_ref[:, a:b]).astype(o_ref.dtype)


def _norm_proj(x2d, nw, w_bf16, outs, tm):
    t, d = x2d.shape
    n = w_bf16.shape[1]
    out_shapes, out_specs = [], []
    for c0, c1, dt, blk in outs:
        if blk:
            nb = (c1 - c0) // blk
            out_shapes.append(jax.ShapeDtypeStruct((nb, t, blk), dt))
            out_specs.append(pl.BlockSpec((nb, tm, blk), lambda i: (0, i, 0)))
        else:
            out_shapes.append(jax.ShapeDtypeStruct((t, c1 - c0), dt))
            out_specs.append(pl.BlockSpec((tm, c1 - c0), lambda i: (i, 0)))
    kern = functools.partial(_norm_proj_kernel,
                             col_splits=tuple((o[0], o[1]) for o in outs),
                             blocked=tuple(o[3] for o in outs))
    return pl.pallas_call(
        kern,
        grid=(t // tm,),
        in_specs=[pl.BlockSpec((tm, d), lambda i: (i, 0)),
                  pl.BlockSpec((1, d), lambda i: (0, 0)),
                  pl.BlockSpec((d, n), lambda i: (0, 0))],
        out_specs=out_specs,
        out_shape=out_shapes,
        compiler_params=_cparams(("parallel",)),
        name="norm_proj",
    )(x2d, nw.reshape(1, d), w_bf16)


def _cmul(ar, ai, xr, xi):
    return ar * xr - ai * xi, ar * xi + ai * xr


def _shift_rows(x, k, rows):
    return jnp.where(rows >= k, pltpu.roll(x, k, axis=0), 0.0)


def _s5_kernel(u_ref, gate_ref, perm_ref, permt_ref, bbd_ref, cbd_ref, ltab_ref, d_ref, wglu_ref, bglu_ref,
               o_ref, xre_scr, xim_scr, carry_scr, y_scr, up_scr, *, tl, nblk):
    sub = tl // SUBLANES
    t_idx = pl.program_id(1)

    @pl.when(t_idx == 0)
    def _():
        carry_scr[...] = jnp.zeros_like(carry_scr)

    rows = lax.broadcasted_iota(jnp.int32, (SUBLANES, LANES), 0)
    perm = perm_ref[...]

    def expand_pieces(cb):
        slot = cb % S5_SLOTS
        held = {}

        def permute():
            held["up"] = _dot(perm, u_ref[cb]).astype(BF16)
            up_scr[cb] = held["up"]

        def columns(q, scr, c0):
            def run():
                b2 = _dot(held["up"], bbd_ref[cb, :, c0:c0 + 2 * LANES])
                for i in range(2):
                    scr[slot, 2 * q + i] = b2[:, i * LANES:(i + 1) * LANES]
            return run

        pieces = [permute]
        for q in range(S5_PAIRS // 2):
            pieces.append(columns(q, xre_scr, 2 * q * LANES))
            pieces.append(columns(q, xim_scr, S5_BLOCK_HALF + 2 * q * LANES))
        return pieces

    def contract_pieces(cb):
        xre_b, xim_b = xre_scr.at[cb % S5_SLOTS], xim_scr.at[cb % S5_SLOTS]
        held = {}

        def real():
            xs = jnp.concatenate([xre_b[n] for n in range(S5_PAIRS)], axis=1).astype(BF16)
            held["y"] = _dot(xs, cbd_ref[cb, 0:S5_BLOCK_HALF, :])

        def imag():
            xs = jnp.concatenate([xim_b[n] for n in range(S5_PAIRS)], axis=1).astype(BF16)
            y_scr[cb] = held["y"] + _dot(xs, cbd_ref[cb, S5_BLOCK_HALF:2 * S5_BLOCK_HALF, :])

        return [real, imag]

    def scan_pieces(cb):
        xre_b, xim_b = xre_scr.at[cb % S5_SLOTS], xim_scr.at[cb % S5_SLOTS]
        half = sub // 2
        pieces = []
        for grp in range(0, S5_PAIRS, 4):
            pairs = list(range(grp, grp + 4))
            held = {}

            def load_lambda(pairs=pairs, held=held):
                held["lam"] = [(ltab_ref[cb, 0, :, n * LANES:(n + 1) * LANES],
                                ltab_ref[cb, 1, :, n * LANES:(n + 1) * LANES]) for n in pairs]
                held["st"] = [(jnp.zeros((SUBLANES, LANES), F32), jnp.zeros((SUBLANES, LANES), F32))
                              for _ in pairs]

            def steps(j0, j1, store, pairs=pairs, held=held):
                lam, st = held["lam"], held["st"]
                for j in range(j0, j1):
                    for q, n in enumerate(pairs):
                        ar, ai = lam[q]
                        xr, xi = st[q]
                        pr, pi = _cmul(ar, ai, xr, xi)
                        xr = pr + xre_b[n, pl.ds(j * SUBLANES, SUBLANES), :]
                        xi = pi + xim_b[n, pl.ds(j * SUBLANES, SUBLANES), :]
                        if store:
                            xre_b[n, pl.ds(j * SUBLANES, SUBLANES), :] = xr
                            xim_b[n, pl.ds(j * SUBLANES, SUBLANES), :] = xi
                        st[q] = (xr, xi)

            def incoming_states(pairs=pairs, held=held):
                st = held["st"]
                for q, n in enumerate(pairs):
                    ls = slice(n * LANES, (n + 1) * LANES)
                    er, ei = st[q]
                    fr = pltpu.roll(jnp.where(rows == SUBLANES - 1, carry_scr[cb, 0, :, ls], er), 1, axis=0)
                    fi = pltpu.roll(jnp.where(rows == SUBLANES - 1, carry_scr[cb, 1, :, ls], ei), 1, axis=0)
                    for lvl, k in enumerate((1, 2, 4)):
                        pr_, pi_ = ltab_ref[cb, 2 + 2 * lvl, :, ls], ltab_ref[cb, 3 + 2 * lvl, :, ls]
                        sr, si = _shift_rows(fr, k, rows), _shift_rows(fi, k, rows)
                        mr, mi = _cmul(pr_, pi_, sr, si)
                        fr, fi = fr + mr, fi + mi
                    st[q] = (fr, fi)

            def save_carry(pairs=pairs, held=held):
                for q, n in enumerate(pairs):
                    ls = slice(n * LANES, (n + 1) * LANES)
                    carry_scr[cb, 0, :, ls] = held["st"][q][0]
                    carry_scr[cb, 1, :, ls] = held["st"][q][1]

            def chain(*fs):
                def run():
                    for f in fs:
                        f()
                return run

            pieces += [chain(load_lambda, functools.partial(steps, 0, half, False)),
                       chain(functools.partial(steps, half, sub, False), incoming_states),
                       functools.partial(steps, 0, half, True),
                       chain(functools.partial(steps, half, sub, True), save_carry)]
        return pieces

    for piece in expand_pieces(0):
        piece()
    for cb in range(nblk + 1):
        mxu = (expand_pieces(cb + 1) if cb + 1 < nblk else []) + (contract_pieces(cb - 1) if cb >= 1 else [])
        vpu = scan_pieces(cb) if cb < nblk else []
        while mxu or vpu:
            for _ in range(-(-len(mxu) // max(len(vpu), 1))):
                if mxu:
                    mxu.pop(0)()
            if vpu:
                vpu.pop(0)()

    y = jnp.concatenate([y_scr[i] for i in range(nblk)], axis=1)
    u = jnp.concatenate([up_scr[i] for i in range(nblk)], axis=1).astype(F32)
    gate = _dot(perm, gate_ref[...])
    y = _gelu_tanh(y + d_ref[...] * u)
    z = _dot(y.astype(BF16), wglu_ref[...]) + bglu_ref[...]
    y = y * _sigmoid(z)
    out = (y * _silu(gate)).astype(BF16)
    o_ref[...] = _dot(permt_ref[...], out).astype(o_ref.dtype)


def _s5_tables(lam_re, lam_im, log_step, b_re, b_im, c_re, c_im, tl):
    groups = lam_re.shape[0]
    nblk = groups // S5_GROUPS_PER_BLOCK
    sub = tl // SUBLANES
    dt = jnp.exp(log_step.astype(F32))[:, None]
    lr, li = lam_re.astype(F32), lam_im.astype(F32)
    mag = jnp.exp(lr * dt)
    ar, ai = mag * jnp.cos(li * dt), mag * jnp.sin(li * dt)
    den = lr * lr + li * li
    nr, ni = ar - 1.0, ai
    cr, ci = (nr * lr + ni * li) / den, (ni * lr - nr * li) / den
    br, bi = b_re.astype(F32), b_im.astype(F32)
    bbar_re = cr[..., None] * br - ci[..., None] * bi
    bbar_im = cr[..., None] * bi + ci[..., None] * br

    gb = S5_GROUPS_PER_BLOCK
    eye = jnp.eye(gb, dtype=F32)

    def expand_blockdiag(w):
        w = w.reshape(nblk, gb, S5_STATE, S5_GROUP)
        return jnp.einsum('ngph,gk->nghkp', w, eye).reshape(nblk, gb * S5_GROUP, gb * S5_STATE)

    bbd = jnp.concatenate([expand_blockdiag(bbar_re), expand_blockdiag(bbar_im)], axis=2).astype(BF16)

    def contract_blockdiag(w):
        w = w.reshape(nblk, gb, S5_GROUP, S5_STATE)
        return jnp.einsum('nghp,gk->ngpkh', w, eye).reshape(nblk, gb * S5_STATE, gb * S5_GROUP)

    cbd = jnp.concatenate([contract_blockdiag(c_re.astype(F32)),
                           -contract_blockdiag(c_im.astype(F32))], axis=1).astype(BF16)

    def lam_pow(k):
        kk = jnp.asarray(k, F32)
        m = jnp.exp(lr * dt * kk)
        return ((m * jnp.cos(li * dt * kk)).reshape(nblk, gb * S5_STATE),
                (m * jnp.sin(li * dt * kk)).reshape(nblk, gb * S5_STATE))

    tab = []
    for k in (1, sub, 2 * sub, 4 * sub):
        pr, pi = lam_pow(k)
        tab += [pr, pi]
    ltab = jnp.stack(tab, axis=1)
    ltab = jnp.broadcast_to(ltab[:, :, None, :], (nblk, 8, SUBLANES, gb * S5_STATE))

    i = jnp.arange(tl)
    src = (i % SUBLANES) * sub + i // SUBLANES
    perm = (src[:, None] == jnp.arange(tl)[None, :]).astype(BF16)
    return bbd, cbd, ltab, perm, perm.T


def _s5_mixer(u_blk, gate, tables, d, w_glu, b_glu, bsz, length, tl):
    bbd, cbd, ltab, perm, permt = tables
    nblk = u_blk.shape[0]
    width = nblk * S5_BLOCK_IN
    nt = length // tl
    kern = functools.partial(_s5_kernel, tl=tl, nblk=nblk)
    const = lambda *shape: pl.BlockSpec(shape, lambda b, t: (0,) * len(shape))
    return pl.pallas_call(
        kern,
        grid=(bsz, nt),
        in_specs=[pl.BlockSpec((nblk, tl, S5_BLOCK_IN), lambda b, t: (0, b * nt + t, 0)),
                  pl.BlockSpec((tl, width), lambda b, t: (b * nt + t, 0)),
                  const(tl, tl), const(tl, tl),
                  const(*bbd.shape), const(*cbd.shape), const(*ltab.shape),
                  const(1, width), const(width, width), const(1, width)],
        out_specs=pl.BlockSpec((tl, width), lambda b, t: (b * nt + t, 0)),
        out_shape=jax.ShapeDtypeStruct((bsz * length, width), BF16),
        scratch_shapes=[pltpu.VMEM((S5_SLOTS, S5_PAIRS, tl, LANES), F32),
                        pltpu.VMEM((S5_SLOTS, S5_PAIRS, tl, LANES), F32),
                        pltpu.VMEM((nblk, 2, SUBLANES, S5_BLOCK_HALF), F32),
                        pltpu.VMEM((nblk, tl, S5_BLOCK_IN), F32),
                        pltpu.VMEM((nblk, tl, S5_BLOCK_IN), BF16)],
        compiler_params=_cparams(("parallel", "arbitrary")),
        name="s5_mixer",
    )(u_blk, gate, perm, permt, bbd, cbd, ltab, d.reshape(1, width).astype(F32),
      w_glu.astype(BF16), b_glu.reshape(1, width).astype(F32))


def _gdn_kernel(qkv_ref, ab_ref, gate_ref, convw_ref, gpar_ref, normw_ref, o_ref,
                ext_scr, g_scr, beta_scr, state_scr, *, tl, qk_heads, v_heads):
    hd = GDN_HEAD_DIM
    qkw = qk_heads * hd
    rep = v_heads // qk_heads
    t_idx = pl.program_id(1)

    @pl.when(t_idx == 0)
    def _():
        state_scr[...] = jnp.zeros_like(state_scr)
        ext_scr[0:GDN_HIST, :] = jnp.zeros((GDN_HIST, ext_scr.shape[1]), BF16)

    ext_scr[GDN_HIST:GDN_HIST + tl, :] = qkv_ref[...]

    def conv_silu(r0, n):
        assert n == GDN_HIST
        si = lax.broadcasted_iota(jnp.int32, (n, 2 * n), 0)
        sj = lax.broadcasted_iota(jnp.int32, (n, 2 * n), 1)
        window = ext_scr[r0:r0 + 2 * n, :]
        conv = convw_ref[CONV_WIDTH - 1:CONV_WIDTH, :] * ext_scr[GDN_HIST + r0:GDN_HIST + r0 + n, :].astype(F32)
        for k in range(1, CONV_WIDTH):
            shift = jnp.where(sj == si + (n - k), 1.0, 0.0).astype(BF16)
            conv = conv + convw_ref[CONV_WIDTH - 1 - k:CONV_WIDTH - k, :] * _dot(shift, window)
        return _silu(conv)

    ab = ab_ref[...]
    xg = ab + gpar_ref[1:2, :]
    softplus = jnp.maximum(xg, 0.0) + jnp.log(1.0 + jnp.exp(-jnp.abs(xg)))
    g_scr[...] = gpar_ref[0:1, :] * softplus
    beta_scr[...] = _sigmoid(ab)

    pr = 2 * CHUNK
    assert hd == pr
    ci = lax.broadcasted_iota(jnp.int32, (pr, pr), 0)
    cj = lax.broadcasted_iota(jnp.int32, (pr, pr), 1)
    same = (ci // CHUNK) == (cj // CHUNK)
    causal = (ci >= cj) & same
    strict = (ci > cj) & same
    tril_f = causal.astype(F32)
    eye = (ci == cj).astype(F32)
    levels = (1, 2, 4, 8, 16, 32)
    level_masks = {b: ((ci // (2 * b)) == (cj // (2 * b))) & ((ci % (2 * b)) >= b) & ((cj % (2 * b)) < b)
                   for b in levels}
    heads_q = range(qk_heads)
    heads_v = range(v_heads)

    def pair_body(r0):
        rs = slice(r0, r0 + pr)
        xc = conv_silu(r0, pr)
        qn, kn = [], []
        for h in heads_q:
            qh = xc[:, h * hd:(h + 1) * hd]
            kh = xc[:, qkw + h * hd:qkw + (h + 1) * hd]
            qn.append(qh * lax.rsqrt(jnp.sum(qh * qh, axis=-1, keepdims=True) + NORM_EPS) * (hd ** -0.5))
            kn.append(kh * lax.rsqrt(jnp.sum(kh * kh, axis=-1, keepdims=True) + NORM_EPS))
        gc = _dot_f32(tril_f, g_scr[rs, :])
        gct = gc.T
        beta_t = beta_scr[rs, :]
        kq = [_dot_nt(jnp.concatenate([kn[h], qn[h]], axis=0).astype(BF16), kn[h].astype(BF16))
              for h in heads_q]
        g_b = [jnp.broadcast_to(gc[:, h:h + 1], (pr, pr)) for h in heads_v]
        beta_b = [jnp.broadcast_to(beta_t[:, v_heads + h:v_heads + h + 1], (pr, pr)) for h in heads_v]
        eg_b = [jnp.exp(g_b[h]) for h in heads_v]
        decay = [jnp.exp(jnp.where(causal, g_b[h] - gct[h:h + 1, :], -jnp.inf)) for h in heads_v]
        lmat = [jnp.where(strict, beta_b[h] * kq[h // rep][:pr] * decay[h], 0.0) for h in heads_v]
        intra = [jnp.where(causal, kq[h // rep][pr:] * decay[h], 0.0).astype(BF16) for h in heads_v]
        x = [eye - jnp.where(level_masks[1], lmat[h], 0.0) for h in heads_v]
        for b in levels[1:]:
            m = level_masks[b]
            xb = [x[h].astype(BF16) for h in heads_v]
            t1 = [_dot(xb[h], jnp.where(m, lmat[h], 0.0).astype(BF16)) for h in heads_v]
            t2 = [_dot(t1[h].astype(BF16), xb[h]) for h in heads_v]
            x = [x[h] - t2[h] for h in heads_v]
        uw = [_dot(x[h].astype(BF16),
                   jnp.concatenate([xc[:, 2 * qkw + h * hd:2 * qkw + (h + 1) * hd] * beta_b[h],
                                    kn[h // rep] * (beta_b[h] * eg_b[h])], axis=1).astype(BF16))
              for h in heads_v]
        qe = [qn[h // rep] * eg_b[h] for h in heads_v]
        s = [state_scr[h] for h in heads_v]
        o_state, v_new = [[] for _ in heads_v], [[] for _ in heads_v]
        for c in range(2):
            cs = slice(c * CHUNK, (c + 1) * CHUNK)
            last = c * CHUNK + CHUNK - 1
            ws = [_dot(jnp.concatenate([uw[h][cs, hd:], qe[h][cs]], axis=0).astype(BF16), s[h].astype(BF16))
                  for h in heads_v]
            for h in heads_v:
                v_new[h].append((uw[h][cs, :hd] - ws[h][:CHUNK]).astype(BF16))
                o_state[h].append(ws[h][CHUNK:])
            g_last = [gc[last:last + 1, h:h + 1] for h in heads_v]
            s = [s[h] * jnp.exp(g_last[h])
                 + _dot_tn((kn[h // rep][cs] * jnp.exp(g_last[h] - g_b[h][cs])).astype(BF16), v_new[h][c])
                 for h in heads_v]
        for h in heads_v:
            state_scr[h] = s[h]
        o = [jnp.concatenate(o_state[h], axis=0) + _dot(intra[h], jnp.concatenate(v_new[h], axis=0))
             for h in heads_v]
        for h in heads_v:
            on = o[h] * lax.rsqrt(jnp.mean(o[h] * o[h], axis=-1, keepdims=True) + NORM_EPS) * normw_ref[...]
            gt = gate_ref[rs, h * hd:(h + 1) * hd].astype(F32)
            o_ref[rs, h * hd:(h + 1) * hd] = (on * _silu(gt)).astype(o_ref.dtype)

    for p in range(tl // pr):
        pair_body(p * pr)
    tail = 2 * SUBLANES
    ext_scr[GDN_HIST - tail:GDN_HIST, :] = ext_scr[GDN_HIST + tl - tail:GDN_HIST + tl, :]


def _gdn_mixer(qkv, ab, gate, conv_w, a_log, dt_bias, norm_w, bsz, length, tl):
    v_heads = a_log.shape[0]
    qk_heads = v_heads // 2
    cw = qkv.shape[1]
    vw = v_heads * GDN_HEAD_DIM
    qkw = qk_heads * GDN_HEAD_DIM
    nt = length // tl
    gpar = jnp.zeros((2, LANES), F32)
    gpar = gpar.at[0, :v_heads].set(-jnp.exp(a_log.astype(F32)))
    gpar = gpar.at[1, :v_heads].set(dt_bias.astype(F32))
    kern = functools.partial(_gdn_kernel, tl=tl, qk_heads=qk_heads, v_heads=v_heads)
    const = lambda *shape: pl.BlockSpec(shape, lambda b, t: (0,) * len(shape))
    return pl.pallas_call(
        kern,
        grid=(bsz, nt),
        in_specs=[pl.BlockSpec((tl, cw), lambda b, t: (b * nt + t, 0)),
                  pl.BlockSpec((tl, LANES), lambda b, t: (b * nt + t, 0)),
                  pl.BlockSpec((tl, vw), lambda b, t: (b * nt + t, 0)),
                  const(CONV_WIDTH, cw), const(2, LANES), const(1, GDN_HEAD_DIM)],
        out_specs=pl.BlockSpec((tl, vw), lambda b, t: (b * nt + t, 0)),
        out_shape=jax.ShapeDtypeStruct((bsz * length, vw), BF16),
        scratch_shapes=[pltpu.VMEM((GDN_HIST + tl, cw), BF16),
                        pltpu.VMEM((tl, LANES), F32),
                        pltpu.VMEM((tl, LANES), F32),
                        pltpu.VMEM((v_heads, GDN_HEAD_DIM, GDN_HEAD_DIM), F32)],
        compiler_params=_cparams(("parallel", "arbitrary")),
        name="gdn_mixer",
    )(qkv, ab, gate, conv_w.astype(F32), gpar, norm_w.reshape(1, GDN_HEAD_DIM).astype(F32))


def _attn_out_kernel(ymix_ref, qx_ref, gx_ref, k_ref, v_ref, wa_ref, wb_ref, x_ref, fw_ref, o_ref,
                     *, final_norm):
    xw = qx_ref.shape[1]
    hd = xw // XATTN_HEADS
    q = qx_ref[...]
    ys = []
    for h in range(XATTN_HEADS):
        hs = slice(h * hd, (h + 1) * hd)
        s = _dot_nt(q[:, hs], k_ref[:, hs]) * (hd ** -0.5)
        s = s - jnp.max(s, axis=-1, keepdims=True)
        p = jnp.exp(s)
        p = p / jnp.sum(p, axis=-1, keepdims=True)
        ys.append(_dot(p.astype(BF16), v_ref[:, hs]))
    yx = jnp.concatenate(ys, axis=1) * _silu(gx_ref[...].astype(F32))
    acc = _dot(ymix_ref[...], wa_ref[...]) + _dot(yx.astype(BF16), wb_ref[...])
    xn = x_ref[...] + acc
    if final_norm:
        xn = xn * lax.rsqrt(jnp.mean(xn * xn, axis=-1, keepdims=True) + NORM_EPS) * fw_ref[...]
    o_ref[...] = xn


def _attn_out(ymix, qx, gx, k, v, w_out, x2d, final_w, bsz, length, tm, final_norm):
    t, d = x2d.shape
    mw = ymix.shape[1]
    xw = qx.shape[1]
    mem = k.shape[0] // bsz
    nt = length // tm
    wa = w_out[:mw].astype(BF16)
    wb = w_out[mw:].astype(BF16)
    kern = functools.partial(_attn_out_kernel, final_norm=final_norm)
    row = lambda width: pl.BlockSpec((tm, width), lambda b, i: (b * nt + i, 0))
    const = lambda *shape: pl.BlockSpec(shape, lambda b, i: (0,) * len(shape))
    return pl.pallas_call(
        kern,
        grid=(bsz, nt),
        in_specs=[row(mw), row(xw), row(xw),
                  pl.BlockSpec((mem, xw), lambda b, i: (b, 0)),
                  pl.BlockSpec((mem, xw), lambda b, i: (b, 0)),
                  const(mw, d), const(xw, d), row(d), const(1, d)],
        out_specs=row(d),
        out_shape=jax.ShapeDtypeStruct((t, d), F32),
        compiler_params=_cparams(("parallel", "parallel")),
        name="attn_out",
    )(ymix, qx, gx, k, v, wa, wb, x2d, final_w.reshape(1, d).astype(F32))


def _pick_tile(n, pref):
    t = pref
    while n % t:
        t //= 2
    return t


def kernel(x, mem, norm_w, w_out, mem_norm_w, w_mem_kv, s5_w_in, s5_lambda_re, s5_lambda_im, s5_log_step, s5_b_re, s5_b_im, s5_c_re, s5_c_im, s5_d, s5_w_glu, s5_b_glu, gdn_w_in, gdn_conv_w, gdn_a_log, gdn_dt_bias, gdn_norm_w, final_norm_w):
    bsz, length, d = x.shape
    mem_tokens = mem.shape[1]
    depth = norm_w.shape[0]
    mix_width = w_out.shape[1]
    xw = mix_width // 4
    tw = mix_width - xw
    v_heads = gdn_a_log.shape[1]
    qkw = (v_heads // 2) * GDN_HEAD_DIM

    t = bsz * length
    x2d = x.reshape(t, d)
    mem2d = mem.reshape(bsz * mem_tokens, d)
    tm_proj = _pick_tile(t, 512)
    tm_mem = _pick_tile(bsz * mem_tokens, 256)
    tl_s5 = _pick_tile(length, 256)
    tl_gdn = _pick_tile(length, 256)
    tm_out = _pick_tile(length, 512)

    for i in range(depth):
        j = i // 2
        kmem, vmem = _norm_proj(mem2d, mem_norm_w[i], w_mem_kv[i].astype(BF16),
                                [(0, xw, BF16, 0), (xw, 2 * xw, BF16, 0)], tm_mem)
        if i % 2 == 0:
            u_blk, gate_mix, qx, gx = _norm_proj(
                x2d, norm_w[i], s5_w_in[j].astype(BF16),
                [(0, tw, BF16, S5_BLOCK_IN), (tw, 2 * tw, BF16, 0),
                 (2 * tw, 2 * tw + xw, BF16, 0), (2 * tw + xw, 2 * tw + 2 * xw, BF16, 0)], tm_proj)
            tables = _s5_tables(s5_lambda_re[j], s5_lambda_im[j], s5_log_step[j], s5_b_re[j], s5_b_im[j],
                                s5_c_re[j], s5_c_im[j], tl_s5)
            ymix = _s5_mixer(u_blk, gate_mix, tables, s5_d[j], s5_w_glu[j], s5_b_glu[j], bsz, length, tl_s5)
        else:
            w = gdn_w_in[j]
            o1 = 2 * qkw + tw
            o3 = o1 + 2 * v_heads
            o4 = o3 + tw
            ab_w = jnp.zeros((d, LANES), w.dtype).at[:, :2 * v_heads].set(w[:, o1:o3])
            wp = jnp.concatenate([w[:, :o1], w[:, o3:], ab_w], axis=1).astype(BF16)
            c_gate = o1
            c_qx = c_gate + tw
            c_gx = c_qx + xw
            c_ab = c_gx + xw
            qkv, gate_mix, qx, gx, ab = _norm_proj(
                x2d, norm_w[i], wp,
                [(0, o1, BF16, 0), (c_gate, c_qx, BF16, 0), (c_qx, c_gx, BF16, 0),
                 (c_gx, c_ab, BF16, 0), (c_ab, c_ab + LANES, F32, 0)], tm_proj)
            ymix = _gdn_mixer(qkv, ab, gate_mix, gdn_conv_w[j], gdn_a_log[j], gdn_dt_bias[j],
                              gdn_norm_w[j], bsz, length, tl_gdn)
        x2d = _attn_out(ymix, qx, gx, kmem, vmem, w_out[i], x2d, final_norm_w, bsz, length, tm_out,
                        final_norm=(i == depth - 1))
    return x2d.reshape(bsz, length, d)
```

```python
import functools
import math

import jax
import jax.numpy as jnp
from jax import lax
from jax.experimental import pallas as pl
from jax.experimental.pallas import tpu as pltpu

F32 = jnp.float32
BF16 = jnp.bfloat16

NORM_EPS = 1e-6
XATTN_HEADS = 4
S5_GROUP = 16
S5_STATE = 64
GDN_HEAD_DIM = 128
CONV_WIDTH = 4
CHUNK = 64
GDN_HIST = 2 * CHUNK

LANES = 128
SUBLANES = 8
VMEM_LIMIT = 56 * 1024 * 1024

S5_GROUPS_PER_BLOCK = 16
S5_BLOCK_IN = S5_GROUPS_PER_BLOCK * S5_GROUP
S5_BLOCK_HALF = S5_GROUPS_PER_BLOCK * S5_STATE
S5_PAIRS = S5_BLOCK_HALF // LANES
S5_SLOTS = 3


def _dot(a, b):
    return jnp.dot(a, b, preferred_element_type=F32)


def _dot_nt(a, b):
    return lax.dot_general(a, b, (((1,), (1,)), ((), ())), preferred_element_type=F32)


def _dot_tn(a, b):
    return lax.dot_general(a, b, (((0,), (0,)), ((), ())), preferred_element_type=F32)


def _dot_f32(a, b):
    return jnp.dot(a, b, preferred_element_type=F32, precision=lax.Precision.HIGHEST)


def _sigmoid(x):
    return 1.0 / (1.0 + jnp.exp(-x))


def _silu(x):
    return x * _sigmoid(x)


def _gelu_tanh(x):
    c = math.sqrt(2.0 / math.pi)
    return 0.5 * x * (1.0 + jnp.tanh(c * (x + 0.044715 * (x * x * x))))


def _cparams(sem):
    return pltpu.CompilerParams(dimension_semantics=sem, vmem_limit_bytes=VMEM_LIMIT)


def _norm_proj_kernel(x_ref, nw_ref, w_ref, *o_refs, col_splits, blocked):
    x = x_ref[...]
    h = x * lax.rsqrt(jnp.mean(x * x, axis=-1, keepdims=True) + NORM_EPS)
    hb = (h * nw_ref[...]).astype(BF16)
    for o_ref, (c0, c1), blk in zip(o_refs, col_splits, blocked):
        if blk:
            for i in range((c1 - c0) // blk):
                o_ref[i] = _dot(hb, w_ref[:, c0 + i * blk:c0 + (i + 1) * blk]).astype(o_ref.dtype)
        else:
            step = 512
            for a in range(c0, c1, step):
                b = min(a + step, c1)
                o_ref[:, a - c0:b - c0] = _dot(hb, w_ref[:, a:b]).astype(o_ref.dtype)


def _norm_proj(x2d, nw, w_bf16, outs, tm):
    t, d = x2d.shape
    n = w_bf16.shape[1]
    out_shapes, out_specs = [], []
    for c0, c1, dt, blk in outs:
        if blk:
            nb = (c1 - c0) // blk
            out_shapes.append(jax.ShapeDtypeStruct((nb, t, blk), dt))
            out_specs.append(pl.BlockSpec((nb, tm, blk), lambda i: (0, i, 0)))
        else:
            out_shapes.append(jax.ShapeDtypeStruct((t, c1 - c0), dt))
            out_specs.append(pl.BlockSpec((tm, c1 - c0), lambda i: (i, 0)))
    kern = functools.partial(_norm_proj_kernel,
                             col_splits=tuple((o[0], o[1]) for o in outs),
                             blocked=tuple(o[3] for o in outs))
    return pl.pallas_call(
        kern,
        grid=(t // tm,),
        in_specs=[pl.BlockSpec((tm, d), lambda i: (i, 0)),
                  pl.BlockSpec((1, d), lambda i: (0, 0)),
                  pl.BlockSpec((d, n), lambda i: (0, 0))],
        out_specs=out_specs,
        out_shape=out_shapes,
        compiler_params=_cparams(("parallel",)),
        name="norm_proj",
    )(x2d, nw.reshape(1, d), w_bf16)


def _cmul(ar, ai, xr, xi):
    return ar * xr - ai * xi, ar * xi + ai * xr


def _shift_rows(x, k, rows):
    return jnp.where(rows >= k, pltpu.roll(x, k, axis=0), 0.0)


def _s5_kernel(u_ref, gate_ref, perm_ref, permt_ref, bbd_ref, cbd_ref, ltab_ref, d_ref, wglu_ref, bglu_ref,
               o_ref, xre_scr, xim_scr, carry_scr, y_scr, up_scr, *, tl, nblk):
    sub = tl // SUBLANES
    t_idx = pl.program_id(1)

    @pl.when(t_idx == 0)
    def _():
        carry_scr[...] = jnp.zeros_like(carry_scr)

    rows = lax.broadcasted_iota(jnp.int32, (SUBLANES, LANES), 0)
    perm = perm_ref[...]

    def expand_pieces(cb):
        slot = cb % S5_SLOTS
        held = {}

        def permute():
            held["up"] = _dot(perm, u_ref[cb]).astype(BF16)
            up_scr[cb] = held["up"]

        def columns(q, scr, c0):
            def run():
                b2 = _dot(held["up"], bbd_ref[cb, :, c0:c0 + 2 * LANES])
                for i in range(2):
                    scr[slot, 2 * q + i] = b2[:, i * LANES:(i + 1) * LANES]
            return run

        pieces = [permute]
        for q in range(S5_PAIRS // 2):
            pieces.append(columns(q, xre_scr, 2 * q * LANES))
            pieces.append(columns(q, xim_scr, S5_BLOCK_HALF + 2 * q * LANES))
        return pieces

    def contract_pieces(cb):
        xre_b, xim_b = xre_scr.at[cb % S5_SLOTS], xim_scr.at[cb % S5_SLOTS]
        held = {}

        def real():
            xs = jnp.concatenate([xre_b[n] for n in range(S5_PAIRS)], axis=1).astype(BF16)
            held["y"] = _dot(xs, cbd_ref[cb, 0:S5_BLOCK_HALF, :])

        def imag():
            xs = jnp.concatenate([xim_b[n] for n in range(S5_PAIRS)], axis=1).astype(BF16)
            y_scr[cb] = held["y"] + _dot(xs, cbd_ref[cb, S5_BLOCK_HALF:2 * S5_BLOCK_HALF, :])

        return [real, imag]

    def scan_pieces(cb):
        xre_b, xim_b = xre_scr.at[cb % S5_SLOTS], xim_scr.at[cb % S5_SLOTS]
        half = sub // 2
        pieces = []
        for grp in range(0, S5_PAIRS, 4):
            pairs = list(range(grp, grp + 4))
            held = {}

            def load_lambda(pairs=pairs, held=held):
                held["lam"] = [(ltab_ref[cb, 0, :, n * LANES:(n + 1) * LANES],
                                ltab_ref[cb, 1, :, n * LANES:(n + 1) * LANES]) for n in pairs]
                held["st"] = [(jnp.zeros((SUBLANES, LANES), F32), jnp.zeros((SUBLANES, LANES), F32))
                              for _ in pairs]

            def steps(j0, j1, store, pairs=pairs, held=held):
                lam, st = held["lam"], held["st"]
                for j in range(j0, j1):
                    for q, n in enumerate(pairs):
                        ar, ai = lam[q]
                        xr, xi = st[q]
                        pr, pi = _cmul(ar, ai, xr, xi)
                        xr = pr + xre_b[n, pl.ds(j * SUBLANES, SUBLANES), :]
                        xi = pi + xim_b[n, pl.ds(j * SUBLANES, SUBLANES), :]
                        if store:
                            xre_b[n, pl.ds(j * SUBLANES, SUBLANES), :] = xr
                            xim_b[n, pl.ds(j * SUBLANES, SUBLANES), :] = xi
                        st[q] = (xr, xi)

            def incoming_states(pairs=pairs, held=held):
                st = held["st"]
                for q, n in enumerate(pairs):
                    ls = slice(n * LANES, (n + 1) * LANES)
                    er, ei = st[q]
                    fr = pltpu.roll(jnp.where(rows == SUBLANES - 1, carry_scr[cb, 0, :, ls], er), 1, axis=0)
                    fi = pltpu.roll(jnp.where(rows == SUBLANES - 1, carry_scr[cb, 1, :, ls], ei), 1, axis=0)
                    for lvl, k in enumerate((1, 2, 4)):
                        pr_, pi_ = ltab_ref[cb, 2 + 2 * lvl, :, ls], ltab_ref[cb, 3 + 2 * lvl, :, ls]
                        sr, si = _shift_rows(fr, k, rows), _shift_rows(fi, k, rows)
                        mr, mi = _cmul(pr_, pi_, sr, si)
                        fr, fi = fr + mr, fi + mi
                    st[q] = (fr, fi)

            def save_carry(pairs=pairs, held=held):
                for q, n in enumerate(pairs):
                    ls = slice(n * LANES, (n + 1) * LANES)
                    carry_scr[cb, 0, :, ls] = held["st"][q][0]
                    carry_scr[cb, 1, :, ls] = held["st"][q][1]

            def chain(*fs):
                def run():
                    for f in fs:
                        f()
                return run

            pieces += [chain(load_lambda, functools.partial(steps, 0, half, False)),
                       chain(functools.partial(steps, half, sub, False), incoming_states),
                       functools.partial(steps, 0, half, True),
                       chain(functools.partial(steps, half, sub, True), save_carry)]
        return pieces

    for piece in expand_pieces(0):
        piece()
    for cb in range(nblk + 1):
        mxu = (expand_pieces(cb + 1) if cb + 1 < nblk else []) + (contract_pieces(cb - 1) if cb >= 1 else [])
        vpu = scan_pieces(cb) if cb < nblk else []
        while mxu or vpu:
            for _ in range(-(-len(mxu) // max(len(vpu), 1))):
                if mxu:
                    mxu.pop(0)()
            if vpu:
                vpu.pop(0)()

    y = jnp.concatenate([y_scr[i] for i in range(nblk)], axis=1)
    u = jnp.concatenate([up_scr[i] for i in range(nblk)], axis=1).astype(F32)
    gate = _dot(perm, gate_ref[...])
    y = _gelu_tanh(y + d_ref[...] * u)
    z = _dot(y.astype(BF16), wglu_ref[...]) + bglu_ref[...]
    y = y * _sigmoid(z)
    out = (y * _silu(gate)).astype(BF16)
    o_ref[...] = _dot(permt_ref[...], out).astype(o_ref.dtype)


def _s5_tables(lam_re, lam_im, log_step, b_re, b_im, c_re, c_im, tl):
    groups = lam_re.shape[0]
    nblk = groups // S5_GROUPS_PER_BLOCK
    sub = tl // SUBLANES
    dt = jnp.exp(log_step.astype(F32))[:, None]
    lr, li = lam_re.astype(F32), lam_im.astype(F32)
    mag = jnp.exp(lr * dt)
    ar, ai = mag * jnp.cos(li * dt), mag * jnp.sin(li * dt)
    den = lr * lr + li * li
    nr, ni = ar - 1.0, ai
    cr, ci = (nr * lr + ni * li) / den, (ni * lr - nr * li) / den
    br, bi = b_re.astype(F32), b_im.astype(F32)
    bbar_re = cr[..., None] * br - ci[..., None] * bi
    bbar_im = cr[..., None] * bi + ci[..., None] * br

    gb = S5_GROUPS_PER_BLOCK
    eye = jnp.eye(gb, dtype=F32)

    def expand_blockdiag(w):
        w = w.reshape(nblk, gb, S5_STATE, S5_GROUP)
        return jnp.einsum('ngph,gk->nghkp', w, eye).reshape(nblk, gb * S5_GROUP, gb * S5_STATE)

    bbd = jnp.concatenate([expand_blockdiag(bbar_re), expand_blockdiag(bbar_im)], axis=2).astype(BF16)

    def contract_blockdiag(w):
        w = w.reshape(nblk, gb, S5_GROUP, S5_STATE)
        return jnp.einsum('nghp,gk->ngpkh', w, eye).reshape(nblk, gb * S5_STATE, gb * S5_GROUP)

    cbd = jnp.concatenate([contract_blockdiag(c_re.astype(F32)),
                           -contract_blockdiag(c_im.astype(F32))], axis=1).astype(BF16)

    def lam_pow(k):
        kk = jnp.asarray(k, F32)
        m = jnp.exp(lr * dt * kk)
        return ((m * jnp.cos(li * dt * kk)).reshape(nblk, gb * S5_STATE),
                (m * jnp.sin(li * dt * kk)).reshape(nblk, gb * S5_STATE))

    tab = []
    for k in (1, sub, 2 * sub, 4 * sub):
        pr, pi = lam_pow(k)
        tab += [pr, pi]
    ltab = jnp.stack(tab, axis=1)
    ltab = jnp.broadcast_to(ltab[:, :, None, :], (nblk, 8, SUBLANES, gb * S5_STATE))

    i = jnp.arange(tl)
    src = (i % SUBLANES) * sub + i // SUBLANES
    perm = (src[:, None] == jnp.arange(tl)[None, :]).astype(BF16)
    return bbd, cbd, ltab, perm, perm.T


def _s5_mixer(u_blk, gate, tables, d, w_glu, b_glu, bsz, length, tl):
    bbd, cbd, ltab, perm, permt = tables
    nblk = u_blk.shape[0]
    width = nblk * S5_BLOCK_IN
    nt = length // tl
    kern = functools.partial(_s5_kernel, tl=tl, nblk=nblk)
    const = lambda *shape: pl.BlockSpec(shape, lambda b, t: (0,) * len(shape))
    return pl.pallas_call(
        kern,
        grid=(bsz, nt),
        in_specs=[pl.BlockSpec((nblk, tl, S5_BLOCK_IN), lambda b, t: (0, b * nt + t, 0)),
                  pl.BlockSpec((tl, width), lambda b, t: (b * nt + t, 0)),
                  const(tl, tl), const(tl, tl),
                  const(*bbd.shape), const(*cbd.shape), const(*ltab.shape),
                  const(1, width), const(width, width), const(1, width)],
        out_specs=pl.BlockSpec((tl, width), lambda b, t: (b * nt + t, 0)),
        out_shape=jax.ShapeDtypeStruct((bsz * length, width), BF16),
        scratch_shapes=[pltpu.VMEM((S5_SLOTS, S5_PAIRS, tl, LANES), F32),
                        pltpu.VMEM((S5_SLOTS, S5_PAIRS, tl, LANES), F32),
                        pltpu.VMEM((nblk, 2, SUBLANES, S5_BLOCK_HALF), F32),
                        pltpu.VMEM((nblk, tl, S5_BLOCK_IN), F32),
                        pltpu.VMEM((nblk, tl, S5_BLOCK_IN), BF16)],
        compiler_params=_cparams(("parallel", "arbitrary")),
        name="s5_mixer",
    )(u_blk, gate, perm, permt, bbd, cbd, ltab, d.reshape(1, width).astype(F32),
      w_glu.astype(BF16), b_glu.reshape(1, width).astype(F32))


def _gdn_kernel(qkv_ref, ab_ref, gate_ref, convw_ref, gpar_ref, normw_ref, o_ref,
                ext_scr, g_scr, beta_scr, state_scr, *, tl, qk_heads, v_heads):
    hd = GDN_HEAD_DIM
    qkw = qk_heads * hd
    rep = v_heads // qk_heads
    t_idx = pl.program_id(1)

    @pl.when(t_idx == 0)
    def _():
        state_scr[...] = jnp.zeros_like(state_scr)
        ext_scr[0:GDN_HIST, :] = jnp.zeros((GDN_HIST, ext_scr.shape[1]), BF16)

    ext_scr[GDN_HIST:GDN_HIST + tl, :] = qkv_ref[...]

    def conv_silu(r0, n):
        assert n == GDN_HIST
        si = lax.broadcasted_iota(jnp.int32, (n, 2 * n), 0)
        sj = lax.broadcasted_iota(jnp.int32, (n, 2 * n), 1)
        window = ext_scr[r0:r0 + 2 * n, :]
        conv = convw_ref[CONV_WIDTH - 1:CONV_WIDTH, :] * ext_scr[GDN_HIST + r0:GDN_HIST + r0 + n, :].astype(F32)
        shifts = jnp.concatenate([jnp.where(sj == si + (n - k), 1.0, 0.0) for k in range(1, CONV_WIDTH)],
                                 axis=0).astype(BF16)
        shifted = _dot(shifts, window)
        for k in range(1, CONV_WIDTH):
            conv = conv + convw_ref[CONV_WIDTH - 1 - k:CONV_WIDTH - k, :] * shifted[(k - 1) * n:k * n]
        return _silu(conv)

    ab = ab_ref[...]
    xg = ab + gpar_ref[1:2, :]
    softplus = jnp.maximum(xg, 0.0) + jnp.log(1.0 + jnp.exp(-jnp.abs(xg)))
    g_scr[...] = gpar_ref[0:1, :] * softplus
    beta_scr[...] = _sigmoid(ab)

    pr = 2 * CHUNK
    assert hd == pr
    ci = lax.broadcasted_iota(jnp.int32, (pr, pr), 0)
    cj = lax.broadcasted_iota(jnp.int32, (pr, pr), 1)
    same = (ci // CHUNK) == (cj // CHUNK)
    causal = (ci >= cj) & same
    strict = (ci > cj) & same
    tril_f = causal.astype(F32)
    eye = (ci == cj).astype(F32)
    levels = (1, 2, 4, 8, 16, 32)
    level_masks = {b: ((ci // (2 * b)) == (cj // (2 * b))) & ((ci % (2 * b)) >= b) & ((cj % (2 * b)) < b)
                   for b in levels}
    heads_q = range(qk_heads)
    heads_v = range(v_heads)

    def pair_inputs(r0):
        xc = conv_silu(r0, pr)
        qn, kn = [], []
        for h in heads_q:
            qh = xc[:, h * hd:(h + 1) * hd]
            kh = xc[:, qkw + h * hd:qkw + (h + 1) * hd]
            qn.append(qh * lax.rsqrt(jnp.sum(qh * qh, axis=-1, keepdims=True) + NORM_EPS) * (hd ** -0.5))
            kn.append(kh * lax.rsqrt(jnp.sum(kh * kh, axis=-1, keepdims=True) + NORM_EPS))
        return xc, qn, kn

    def pair_body(r0, inputs):
        rs = slice(r0, r0 + pr)
        xc, qn, kn = inputs
        gc = _dot_f32(tril_f, g_scr[rs, :])
        gct = gc.T
        beta_t = beta_scr[rs, :]
        kq = [_dot_nt(jnp.concatenate([kn[h], qn[h]], axis=0).astype(BF16), kn[h].astype(BF16))
              for h in heads_q]
        g_b = [jnp.broadcast_to(gc[:, h:h + 1], (pr, pr)) for h in heads_v]
        beta_b = [jnp.broadcast_to(beta_t[:, v_heads + h:v_heads + h + 1], (pr, pr)) for h in heads_v]
        eg_b = [jnp.exp(g_b[h]) for h in heads_v]
        decay = [jnp.exp(jnp.where(causal, g_b[h] - gct[h:h + 1, :], -jnp.inf)) for h in heads_v]
        lmat = [jnp.where(strict, beta_b[h] * kq[h // rep][:pr] * decay[h], 0.0) for h in heads_v]
        intra = [jnp.where(causal, kq[h // rep][pr:] * decay[h], 0.0).astype(BF16) for h in heads_v]
        x = [eye - jnp.where(level_masks[1], lmat[h], 0.0) for h in heads_v]
        for b in levels[1:]:
            m = level_masks[b]
            xb = [x[h].astype(BF16) for h in heads_v]
            t1 = [_dot(xb[h], jnp.where(m, lmat[h], 0.0).astype(BF16)) for h in heads_v]
            t2 = [_dot(t1[h].astype(BF16), xb[h]) for h in heads_v]
            x = [x[h] - t2[h] for h in heads_v]
        uw = [_dot(x[h].astype(BF16),
                   jnp.concatenate([xc[:, 2 * qkw + h * hd:2 * qkw + (h + 1) * hd] * beta_b[h],
                                    kn[h // rep] * (beta_b[h] * eg_b[h])], axis=1).astype(BF16))
              for h in heads_v]
        qe = [qn[h // rep] * eg_b[h] for h in heads_v]
        s = [state_scr[h] for h in heads_v]
        o_state, v_new = [[] for _ in heads_v], [[] for _ in heads_v]
        for c in range(2):
            cs = slice(c * CHUNK, (c + 1) * CHUNK)
            last = c * CHUNK + CHUNK - 1
            ws = [_dot(jnp.concatenate([uw[h][cs, hd:], qe[h][cs]], axis=0).astype(BF16), s[h].astype(BF16))
                  for h in heads_v]
            for h in heads_v:
                v_new[h].append((uw[h][cs, :hd] - ws[h][:CHUNK]).astype(BF16))
                o_state[h].append(ws[h][CHUNK:])
            g_last = [gc[last:last + 1, h:h + 1] for h in heads_v]
            s = [s[h] * jnp.exp(g_last[h])
                 + _dot_tn((kn[h // rep][cs] * jnp.exp(g_last[h] - g_b[h][cs])).astype(BF16), v_new[h][c])
                 for h in heads_v]
        for h in heads_v:
            state_scr[h] = s[h]
        o = [jnp.concatenate(o_state[h], axis=0) + _dot(intra[h], jnp.concatenate(v_new[h], axis=0))
             for h in heads_v]
        for h in heads_v:
            on = o[h] * lax.rsqrt(jnp.mean(o[h] * o[h], axis=-1, keepdims=True) + NORM_EPS) * normw_ref[...]
            gt = gate_ref[rs, h * hd:(h + 1) * hd].astype(F32)
            o_ref[rs, h * hd:(h + 1) * hd] = (on * _silu(gt)).astype(o_ref.dtype)

    npairs = tl // pr
    inputs = pair_inputs(0)
    for p in range(npairs):
        nxt = pair_inputs((p + 1) * pr) if p + 1 < npairs else None
        pair_body(p * pr, inputs)
        inputs = nxt
    tail = 2 * SUBLANES
    ext_scr[GDN_HIST - tail:GDN_HIST, :] = ext_scr[GDN_HIST + tl - tail:GDN_HIST + tl, :]


def _gdn_mixer(qkv, ab, gate, conv_w, a_log, dt_bias, norm_w, bsz, length, tl):
    v_heads = a_log.shape[0]
    qk_heads = v_heads // 2
    cw = qkv.shape[1]
    vw = v_heads * GDN_HEAD_DIM
    qkw = qk_heads * GDN_HEAD_DIM
    nt = length // tl
    gpar = jnp.zeros((2, LANES), F32)
    gpar = gpar.at[0, :v_heads].set(-jnp.exp(a_log.astype(F32)))
    gpar = gpar.at[1, :v_heads].set(dt_bias.astype(F32))
    kern = functools.partial(_gdn_kernel, tl=tl, qk_heads=qk_heads, v_heads=v_heads)
    const = lambda *shape: pl.BlockSpec(shape, lambda b, t: (0,) * len(shape))
    return pl.pallas_call(
        kern,
        grid=(bsz, nt),
        in_specs=[pl.BlockSpec((tl, cw), lambda b, t: (b * nt + t, 0)),
                  pl.BlockSpec((tl, LANES), lambda b, t: (b * nt + t, 0)),
                  pl.BlockSpec((tl, vw), lambda b, t: (b * nt + t, 0)),
                  const(CONV_WIDTH, cw), const(2, LANES), const(1, GDN_HEAD_DIM)],
        out_specs=pl.BlockSpec((tl, vw), lambda b, t: (b * nt + t, 0)),
        out_shape=jax.ShapeDtypeStruct((bsz * length, vw), BF16),
        scratch_shapes=[pltpu.VMEM((GDN_HIST + tl, cw), BF16),
                        pltpu.VMEM((tl, LANES), F32),
                        pltpu.VMEM((tl, LANES), F32),
                        pltpu.VMEM((v_heads, GDN_HEAD_DIM, GDN_HEAD_DIM), F32)],
        compiler_params=_cparams(("parallel", "arbitrary")),
        name="gdn_mixer",
    )(qkv, ab, gate, conv_w.astype(F32), gpar, norm_w.reshape(1, GDN_HEAD_DIM).astype(F32))


def _attn_out_kernel(ymix_ref, qx_ref, gx_ref, k_ref, v_ref, wa_ref, wb_ref, x_ref, fw_ref, o_ref,
                     *, final_norm):
    xw = qx_ref.shape[1]
    hd = xw // XATTN_HEADS
    q = qx_ref[...]
    ys = []
    for h in range(XATTN_HEADS):
        hs = slice(h * hd, (h + 1) * hd)
        s = _dot_nt(q[:, hs], k_ref[:, hs]) * (hd ** -0.5)
        s = s - jnp.max(s, axis=-1, keepdims=True)
        p = jnp.exp(s)
        p = p / jnp.sum(p, axis=-1, keepdims=True)
        ys.append(_dot(p.astype(BF16), v_ref[:, hs]))
    yx = jnp.concatenate(ys, axis=1) * _silu(gx_ref[...].astype(F32))
    acc = _dot(ymix_ref[...], wa_ref[...]) + _dot(yx.astype(BF16), wb_ref[...])
    xn = x_ref[...] + acc
    if final_norm:
        xn = xn * lax.rsqrt(jnp.mean(xn * xn, axis=-1, keepdims=True) + NORM_EPS) * fw_ref[...]
    o_ref[...] = xn


def _attn_out(ymix, qx, gx, k, v, w_out, x2d, final_w, bsz, length, tm, final_norm):
    t, d = x2d.shape
    mw = ymix.shape[1]
    xw = qx.shape[1]
    mem = k.shape[0] // bsz
    nt = length // tm
    wa = w_out[:mw].astype(BF16)
    wb = w_out[mw:].astype(BF16)
    kern = functools.partial(_attn_out_kernel, final_norm=final_norm)
    row = lambda width: pl.BlockSpec((tm, width), lambda b, i: (b * nt + i, 0))
    const = lambda *shape: pl.BlockSpec(shape, lambda b, i: (0,) * len(shape))
    return pl.pallas_call(
        kern,
        grid=(bsz, nt),
        in_specs=[row(mw), row(xw), row(xw),
                  pl.BlockSpec((mem, xw), lambda b, i: (b, 0)),
                  pl.BlockSpec((mem, xw), lambda b, i: (b, 0)),
                  const(mw, d), const(xw, d), row(d), const(1, d)],
        out_specs=row(d),
        out_shape=jax.ShapeDtypeStruct((t, d), F32),
        compiler_params=_cparams(("parallel", "parallel")),
        name="attn_out",
    )(ymix, qx, gx, k, v, wa, wb, x2d, final_w.reshape(1, d).astype(F32))


def _pick_tile(n, pref):
    t = pref
    while n % t:
        t //= 2
    return t


def kernel(x, mem, norm_w, w_out, mem_norm_w, w_mem_kv, s5_w_in, s5_lambda_re, s5_lambda_im, s5_log_step, s5_b_re, s5_b_im, s5_c_re, s5_c_im, s5_d, s5_w_glu, s5_b_glu, gdn_w_in, gdn_conv_w, gdn_a_log, gdn_dt_bias, gdn_norm_w, final_norm_w):
    bsz, length, d = x.shape
    mem_tokens = mem.shape[1]
    depth = norm_w.shape[0]
    mix_width = w_out.shape[1]
    xw = mix_width // 4
    tw = mix_width - xw
    v_heads = gdn_a_log.shape[1]
    qkw = (v_heads // 2) * GDN_HEAD_DIM

    t = bsz * length
    x2d = x.reshape(t, d)
    mem2d = mem.reshape(bsz * mem_tokens, d)
    tm_proj = _pick_tile(t, 512)
    tm_mem = _pick_tile(bsz * mem_tokens, 256)
    tl_s5 = _pick_tile(length, 256)
    tl_gdn = _pick_tile(length, 512)
    tm_out = _pick_tile(length, 512)

    for i in range(depth):
        j = i // 2
        kmem, vmem = _norm_proj(mem2d, mem_norm_w[i], w_mem_kv[i].astype(BF16),
                                [(0, xw, BF16, 0), (xw, 2 * xw, BF16, 0)], tm_mem)
        if i % 2 == 0:
            u_blk, gate_mix, qx, gx = _norm_proj(
                x2d, norm_w[i], s5_w_in[j].astype(BF16),
                [(0, tw, BF16, S5_BLOCK_IN), (tw, 2 * tw, BF16, 0),
                 (2 * tw, 2 * tw + xw, BF16, 0), (2 * tw + xw, 2 * tw + 2 * xw, BF16, 0)], tm_proj)
            tables = _s5_tables(s5_lambda_re[j], s5_lambda_im[j], s5_log_step[j], s5_b_re[j], s5_b_im[j],
                                s5_c_re[j], s5_c_im[j], tl_s5)
            ymix = _s5_mixer(u_blk, gate_mix, tables, s5_d[j], s5_w_glu[j], s5_b_glu[j], bsz, length, tl_s5)
        else:
            w = gdn_w_in[j]
            o1 = 2 * qkw + tw
            o3 = o1 + 2 * v_heads
            o4 = o3 + tw
            ab_w = jnp.zeros((d, LANES), w.dtype).at[:, :2 * v_heads].set(w[:, o1:o3])
            wp = jnp.concatenate([w[:, :o1], w[:, o3:], ab_w], axis=1).astype(BF16)
            c_gate = o1
            c_qx = c_gate + tw
            c_gx = c_qx + xw
            c_ab = c_gx + xw
            qkv, gate_mix, qx, gx, ab = _norm_proj(
                x2d, norm_w[i], wp,
                [(0, o1, BF16, 0), (c_gate, c_qx, BF16, 0), (c_qx, c_gx, BF16, 0),
                 (c_gx, c_ab, BF16, 0), (c_ab, c_ab + LANES, F32, 0)], tm_proj)
            ymix = _gdn_mixer(qkv, ab, gate_mix, gdn_conv_w[j], gdn_a_log[j], gdn_dt_bias[j],
                              gdn_norm_w[j], bsz, length, tl_gdn)
        x2d = _attn_out(ymix, qx, gx, kmem, vmem, w_out[i], x2d, final_norm_w, bsz, length, tm_out,
                        final_norm=(i == depth - 1))
    return x2d.reshape(bsz, length, d)
```

```python
import functools
import math

import jax
import jax.numpy as jnp
from jax import lax
from jax.experimental import pallas as pl
from jax.experimental.pallas import tpu as pltpu

F32 = jnp.float32
BF16 = jnp.bfloat16

NORM_EPS = 1e-6
XATTN_HEADS = 4
S5_GROUP = 16
S5_STATE = 64
GDN_HEAD_DIM = 128
CONV_WIDTH = 4
CHUNK = 64
GDN_HIST = 2 * CHUNK

LANES = 128
SUBLANES = 8
VMEM_LIMIT = 56 * 1024 * 1024

S5_GROUPS_PER_BLOCK = 16
S5_BLOCK_IN = S5_GROUPS_PER_BLOCK * S5_GROUP
S5_BLOCK_HALF = S5_GROUPS_PER_BLOCK * S5_STATE
S5_PAIRS = S5_BLOCK_HALF // LANES
S5_SLOTS = 3


def _dot(a, b):
    return jnp.dot(a, b, preferred_element_type=F32)


def _dot_nt(a, b):
    return lax.dot_general(a, b, (((1,), (1,)), ((), ())), preferred_element_type=F32)


def _dot_tn(a, b):
    return lax.dot_general(a, b, (((0,), (0,)), ((), ())), preferred_element_type=F32)


def _dot_f32(a, b):
    return jnp.dot(a, b, preferred_element_type=F32, precision=lax.Precision.HIGHEST)


def _sigmoid(x):
    return 1.0 / (1.0 + jnp.exp(-x))


def _silu(x):
    return x * _sigmoid(x)


def _gelu_tanh(x):
    c = math.sqrt(2.0 / math.pi)
    return 0.5 * x * (1.0 + jnp.tanh(c * (x + 0.044715 * (x * x * x))))


def _cparams(sem):
    return pltpu.CompilerParams(dimension_semantics=sem, vmem_limit_bytes=VMEM_LIMIT)


def _norm_proj_kernel(x_ref, nw_ref, w_ref, *o_refs, col_splits, blocked):
    x = x_ref[...]
    h = x * lax.rsqrt(jnp.mean(x * x, axis=-1, keepdims=True) + NORM_EPS)
    hb = (h * nw_ref[...]).astype(BF16)
    for o_ref, (c0, c1), blk in zip(o_refs, col_splits, blocked):
        if blk:
            for i in range((c1 - c0) // blk):
                o_ref[i] = _dot(hb, w_ref[:, c0 + i * blk:c0 + (i + 1) * blk]).astype(o_ref.dtype)
        else:
            step = 512
            for a in range(c0, c1, step):
                b = min(a + step, c1)
                o_ref[:, a - c0:b - c0] = _dot(hb, w_ref[:, a:b]).astype(o_ref.dtype)


def _norm_proj(x2d, nw, w_bf16, outs, tm):
    t, d = x2d.shape
    n = w_bf16.shape[1]
    out_shapes, out_specs = [], []
    for c0, c1, dt, blk in outs:
        if blk:
            nb = (c1 - c0) // blk
            out_shapes.append(jax.ShapeDtypeStruct((nb, t, blk), dt))
            out_specs.append(pl.BlockSpec((nb, tm, blk), lambda i: (0, i, 0)))
        else:
            out_shapes.append(jax.ShapeDtypeStruct((t, c1 - c0), dt))
            out_specs.append(pl.BlockSpec((tm, c1 - c0), lambda i: (i, 0)))
    kern = functools.partial(_norm_proj_kernel,
                             col_splits=tuple((o[0], o[1]) for o in outs),
                             blocked=tuple(o[3] for o in outs))
    return pl.pallas_call(
        kern,
        grid=(t // tm,),
        in_specs=[pl.BlockSpec((tm, d), lambda i: (i, 0)),
                  pl.BlockSpec((1, d), lambda i: (0, 0)),
                  pl.BlockSpec((d, n), lambda i: (0, 0))],
        out_specs=out_specs,
        out_shape=out_shapes,
        compiler_params=_cparams(("parallel",)),
        name="norm_proj",
    )(x2d, nw.reshape(1, d), w_bf16)


def _cmul(ar, ai, xr, xi):
    return ar * xr - ai * xi, ar * xi + ai * xr


def _shift_rows(x, k, rows):
    return jnp.where(rows >= k, pltpu.roll(x, k, axis=0), 0.0)


def _s5_kernel(u_ref, gate_ref, perm_ref, permt_ref, bbd_ref, cbd_ref, ltab_ref, d_ref, wglu_ref, bglu_ref,
               o_ref, xre_scr, xim_scr, carry_scr, y_scr, up_scr, *, tl, nblk):
    sub = tl // SUBLANES
    t_idx = pl.program_id(1)

    @pl.when(t_idx == 0)
    def _():
        carry_scr[...] = jnp.zeros_like(carry_scr)

    rows = lax.broadcasted_iota(jnp.int32, (SUBLANES, LANES), 0)
    perm = perm_ref[...]

    def expand_pieces(cb):
        slot = cb % S5_SLOTS
        held = {}

        def permute():
            held["up"] = _dot(perm, u_ref[cb]).astype(BF16)
            up_scr[cb] = held["up"]

        def columns(q, scr, c0):
            def run():
                b2 = _dot(held["up"], bbd_ref[cb, :, c0:c0 + 2 * LANES])
                for i in range(2):
                    scr[slot, 2 * q + i] = b2[:, i * LANES:(i + 1) * LANES]
            return run

        pieces = [permute]
        for q in range(S5_PAIRS // 2):
            pieces.append(columns(q, xre_scr, 2 * q * LANES))
            pieces.append(columns(q, xim_scr, S5_BLOCK_HALF + 2 * q * LANES))
        return pieces

    def contract_pieces(cb):
        xre_b, xim_b = xre_scr.at[cb % S5_SLOTS], xim_scr.at[cb % S5_SLOTS]
        held = {}

        def real():
            xs = jnp.concatenate([xre_b[n] for n in range(S5_PAIRS)], axis=1).astype(BF16)
            held["y"] = _dot(xs, cbd_ref[cb, 0:S5_BLOCK_HALF, :])

        def imag():
            xs = jnp.concatenate([xim_b[n] for n in range(S5_PAIRS)], axis=1).astype(BF16)
            y_scr[cb] = held["y"] + _dot(xs, cbd_ref[cb, S5_BLOCK_HALF:2 * S5_BLOCK_HALF, :])

        return [real, imag]

    def scan_pieces(cb):
        xre_b, xim_b = xre_scr.at[cb % S5_SLOTS], xim_scr.at[cb % S5_SLOTS]
        half = sub // 2
        pieces = []
        for grp in range(0, S5_PAIRS, 4):
            pairs = list(range(grp, grp + 4))
            held = {}

            def load_lambda(pairs=pairs, held=held):
                held["lam"] = [(ltab_ref[cb, 0, :, n * LANES:(n + 1) * LANES],
                                ltab_ref[cb, 1, :, n * LANES:(n + 1) * LANES]) for n in pairs]
                held["st"] = [(jnp.zeros((SUBLANES, LANES), F32), jnp.zeros((SUBLANES, LANES), F32))
                              for _ in pairs]

            def steps(j0, j1, store, pairs=pairs, held=held):
                lam, st = held["lam"], held["st"]
                for j in range(j0, j1):
                    for q, n in enumerate(pairs):
                        ar, ai = lam[q]
                        xr, xi = st[q]
                        pr, pi = _cmul(ar, ai, xr, xi)
                        xr = pr + xre_b[n, pl.ds(j * SUBLANES, SUBLANES), :]
                        xi = pi + xim_b[n, pl.ds(j * SUBLANES, SUBLANES), :]
                        if store:
                            xre_b[n, pl.ds(j * SUBLANES, SUBLANES), :] = xr
                            xim_b[n, pl.ds(j * SUBLANES, SUBLANES), :] = xi
                        st[q] = (xr, xi)

            def incoming_states(pairs=pairs, held=held):
                st = held["st"]
                for q, n in enumerate(pairs):
                    ls = slice(n * LANES, (n + 1) * LANES)
                    er, ei = st[q]
                    fr = pltpu.roll(jnp.where(rows == SUBLANES - 1, carry_scr[cb, 0, :, ls], er), 1, axis=0)
                    fi = pltpu.roll(jnp.where(rows == SUBLANES - 1, carry_scr[cb, 1, :, ls], ei), 1, axis=0)
                    for lvl, k in enumerate((1, 2, 4)):
                        pr_, pi_ = ltab_ref[cb, 2 + 2 * lvl, :, ls], ltab_ref[cb, 3 + 2 * lvl, :, ls]
                        sr, si = _shift_rows(fr, k, rows), _shift_rows(fi, k, rows)
                        mr, mi = _cmul(pr_, pi_, sr, si)
                        fr, fi = fr + mr, fi + mi
                    st[q] = (fr, fi)

            def save_carry(pairs=pairs, held=held):
                for q, n in enumerate(pairs):
                    ls = slice(n * LANES, (n + 1) * LANES)
                    carry_scr[cb, 0, :, ls] = held["st"][q][0]
                    carry_scr[cb, 1, :, ls] = held["st"][q][1]

            def chain(*fs):
                def run():
                    for f in fs:
                        f()
                return run

            pieces += [chain(load_lambda, functools.partial(steps, 0, half, False)),
                       chain(functools.partial(steps, half, sub, False), incoming_states),
                       functools.partial(steps, 0, half, True),
                       chain(functools.partial(steps, half, sub, True), save_carry)]
        return pieces

    for piece in expand_pieces(0):
        piece()
    for cb in range(nblk + 1):
        mxu = (expand_pieces(cb + 1) if cb + 1 < nblk else []) + (contract_pieces(cb - 1) if cb >= 1 else [])
        vpu = scan_pieces(cb) if cb < nblk else []
        while mxu or vpu:
            for _ in range(-(-len(mxu) // max(len(vpu), 1))):
                if mxu:
                    mxu.pop(0)()
            if vpu:
                vpu.pop(0)()

    y = jnp.concatenate([y_scr[i] for i in range(nblk)], axis=1)
    u = jnp.concatenate([up_scr[i] for i in range(nblk)], axis=1).astype(F32)
    gate = _dot(perm, gate_ref[...])
    y = _gelu_tanh(y + d_ref[...] * u)
    z = _dot(y.astype(BF16), wglu_ref[...]) + bglu_ref[...]
    y = y * _sigmoid(z)
    out = (y * _silu(gate)).astype(BF16)
    o_ref[...] = _dot(permt_ref[...], out).astype(o_ref.dtype)


def _s5_tables(lam_re, lam_im, log_step, b_re, b_im, c_re, c_im, tl):
    groups = lam_re.shape[0]
    nblk = groups // S5_GROUPS_PER_BLOCK
    sub = tl // SUBLANES
    dt = jnp.exp(log_step.astype(F32))[:, None]
    lr, li = lam_re.astype(F32), lam_im.astype(F32)
    mag = jnp.exp(lr * dt)
    ar, ai = mag * jnp.cos(li * dt), mag * jnp.sin(li * dt)
    den = lr * lr + li * li
    nr, ni = ar - 1.0, ai
    cr, ci = (nr * lr + ni * li) / den, (ni * lr - nr * li) / den
    br, bi = b_re.astype(F32), b_im.astype(F32)
    bbar_re = cr[..., None] * br - ci[..., None] * bi
    bbar_im = cr[..., None] * bi + ci[..., None] * br

    gb = S5_GROUPS_PER_BLOCK
    eye = jnp.eye(gb, dtype=F32)

    def expand_blockdiag(w):
        w = w.reshape(nblk, gb, S5_STATE, S5_GROUP)
        return jnp.einsum('ngph,gk->nghkp', w, eye).reshape(nblk, gb * S5_GROUP, gb * S5_STATE)

    bbd = jnp.concatenate([expand_blockdiag(bbar_re), expand_blockdiag(bbar_im)], axis=2).astype(BF16)

    def contract_blockdiag(w):
        w = w.reshape(nblk, gb, S5_GROUP, S5_STATE)
        return jnp.einsum('nghp,gk->ngpkh', w, eye).reshape(nblk, gb * S5_STATE, gb * S5_GROUP)

    cbd = jnp.concatenate([contract_blockdiag(c_re.astype(F32)),
                           -contract_blockdiag(c_im.astype(F32))], axis=1).astype(BF16)

    def lam_pow(k):
        kk = jnp.asarray(k, F32)
        m = jnp.exp(lr * dt * kk)
        return ((m * jnp.cos(li * dt * kk)).reshape(nblk, gb * S5_STATE),
                (m * jnp.sin(li * dt * kk)).reshape(nblk, gb * S5_STATE))

    tab = []
    for k in (1, sub, 2 * sub, 4 * sub):
        pr, pi = lam_pow(k)
        tab += [pr, pi]
    ltab = jnp.stack(tab, axis=1)
    ltab = jnp.broadcast_to(ltab[:, :, None, :], (nblk, 8, SUBLANES, gb * S5_STATE))

    i = jnp.arange(tl)
    src = (i % SUBLANES) * sub + i // SUBLANES
    perm = (src[:, None] == jnp.arange(tl)[None, :]).astype(BF16)
    return bbd, cbd, ltab, perm, perm.T


def _s5_mixer(u_blk, gate, tables, d, w_glu, b_glu, bsz, length, tl):
    bbd, cbd, ltab, perm, permt = tables
    nblk = u_blk.shape[0]
    width = nblk * S5_BLOCK_IN
    nt = length // tl
    kern = functools.partial(_s5_kernel, tl=tl, nblk=nblk)
    const = lambda *shape: pl.BlockSpec(shape, lambda b, t: (0,) * len(shape))
    return pl.pallas_call(
        kern,
        grid=(bsz, nt),
        in_specs=[pl.BlockSpec((nblk, tl, S5_BLOCK_IN), lambda b, t: (0, b * nt + t, 0)),
                  pl.BlockSpec((tl, width), lambda b, t: (b * nt + t, 0)),
                  const(tl, tl), const(tl, tl),
                  const(*bbd.shape), const(*cbd.shape), const(*ltab.shape),
                  const(1, width), const(width, width), const(1, width)],
        out_specs=pl.BlockSpec((tl, width), lambda b, t: (b * nt + t, 0)),
        out_shape=jax.ShapeDtypeStruct((bsz * length, width), BF16),
        scratch_shapes=[pltpu.VMEM((S5_SLOTS, S5_PAIRS, tl, LANES), F32),
                        pltpu.VMEM((S5_SLOTS, S5_PAIRS, tl, LANES), F32),
                        pltpu.VMEM((nblk, 2, SUBLANES, S5_BLOCK_HALF), F32),
                        pltpu.VMEM((nblk, tl, S5_BLOCK_IN), F32),
                        pltpu.VMEM((nblk, tl, S5_BLOCK_IN), BF16)],
        compiler_params=_cparams(("parallel", "arbitrary")),
        name="s5_mixer",
    )(u_blk, gate, perm, permt, bbd, cbd, ltab, d.reshape(1, width).astype(F32),
      w_glu.astype(BF16), b_glu.reshape(1, width).astype(F32))


def _gdn_kernel(qkv_ref, ab_ref, gate_ref, convw_ref, gpar_ref, normw_ref, o_ref,
                ext_scr, g_scr, beta_scr, state_scr, *, tl, qk_heads, v_heads):
    hd = GDN_HEAD_DIM
    qkw = qk_heads * hd
    rep = v_heads // qk_heads
    t_idx = pl.program_id(1)

    @pl.when(t_idx == 0)
    def _():
        state_scr[...] = jnp.zeros_like(state_scr)
        ext_scr[0:GDN_HIST, :] = jnp.zeros((GDN_HIST, ext_scr.shape[1]), BF16)

    ext_scr[GDN_HIST:GDN_HIST + tl, :] = qkv_ref[...]

    def conv_silu(r0, n):
        assert n == GDN_HIST
        si = lax.broadcasted_iota(jnp.int32, (n, 2 * n), 0)
        sj = lax.broadcasted_iota(jnp.int32, (n, 2 * n), 1)
        window = ext_scr[r0:r0 + 2 * n, :]
        conv = convw_ref[CONV_WIDTH - 1:CONV_WIDTH, :] * ext_scr[GDN_HIST + r0:GDN_HIST + r0 + n, :].astype(F32)
        shifts = jnp.concatenate([jnp.where(sj == si + (n - k), 1.0, 0.0) for k in range(1, CONV_WIDTH)],
                                 axis=0).astype(BF16)
        shifted = _dot(shifts, window)
        for k in range(1, CONV_WIDTH):
            conv = conv + convw_ref[CONV_WIDTH - 1 - k:CONV_WIDTH - k, :] * shifted[(k - 1) * n:k * n]
        return _silu(conv)

    ab = ab_ref[...]
    xg = ab + gpar_ref[1:2, :]
    softplus = jnp.maximum(xg, 0.0) + jnp.log(1.0 + jnp.exp(-jnp.abs(xg)))
    g_scr[...] = gpar_ref[0:1, :] * softplus
    beta_scr[...] = _sigmoid(ab)

    pr = 2 * CHUNK
    assert hd == pr
    ci = lax.broadcasted_iota(jnp.int32, (pr, pr), 0)
    cj = lax.broadcasted_iota(jnp.int32, (pr, pr), 1)
    same = (ci // CHUNK) == (cj // CHUNK)
    causal = (ci >= cj) & same
    strict = (ci > cj) & same
    tril_f = causal.astype(F32)
    eye = (ci == cj).astype(F32)
    levels = (1, 2, 4, 8, 16, 32)
    level_masks = {b: ((ci // (2 * b)) == (cj // (2 * b))) & ((ci % (2 * b)) >= b) & ((cj % (2 * b)) < b)
                   for b in levels}
    heads_q = range(qk_heads)
    heads_v = range(v_heads)

    def pair_inputs(r0):
        xc = conv_silu(r0, pr)
        qn, kn = [], []
        for h in heads_q:
            qh = xc[:, h * hd:(h + 1) * hd]
            kh = xc[:, qkw + h * hd:qkw + (h + 1) * hd]
            qn.append(qh * lax.rsqrt(jnp.sum(qh * qh, axis=-1, keepdims=True) + NORM_EPS) * (hd ** -0.5))
            kn.append(kh * lax.rsqrt(jnp.sum(kh * kh, axis=-1, keepdims=True) + NORM_EPS))
        return xc, qn, kn

    def pair_body(r0, inputs):
        rs = slice(r0, r0 + pr)
        xc, qn, kn = inputs
        gc = _dot_f32(tril_f, g_scr[rs, :])
        gct = gc.T
        beta_t = beta_scr[rs, :]
        kq = [_dot_nt(jnp.concatenate([kn[h], qn[h]], axis=0).astype(BF16), kn[h].astype(BF16))
              for h in heads_q]
        g_b = [jnp.broadcast_to(gc[:, h:h + 1], (pr, pr)) for h in heads_v]
        beta_b = [jnp.broadcast_to(beta_t[:, v_heads + h:v_heads + h + 1], (pr, pr)) for h in heads_v]
        eg_b = [jnp.exp(g_b[h]) for h in heads_v]
        decay = [jnp.exp(jnp.where(causal, g_b[h] - gct[h:h + 1, :], -jnp.inf)) for h in heads_v]
        lmat = [jnp.where(strict, beta_b[h] * kq[h // rep][:pr] * decay[h], 0.0) for h in heads_v]
        intra = [jnp.where(causal, kq[h // rep][pr:] * decay[h], 0.0).astype(BF16) for h in heads_v]
        x = [eye - jnp.where(level_masks[1], lmat[h], 0.0) for h in heads_v]
        for b in levels[1:]:
            m = level_masks[b]
            xb = [x[h].astype(BF16) for h in heads_v]
            t1 = [_dot(xb[h], jnp.where(m, lmat[h], 0.0).astype(BF16)) for h in heads_v]
            t2 = [_dot(t1[h].astype(BF16), xb[h]) for h in heads_v]
            x = [x[h] - t2[h] for h in heads_v]
        uw = [_dot(x[h].astype(BF16),
                   jnp.concatenate([xc[:, 2 * qkw + h * hd:2 * qkw + (h + 1) * hd] * beta_b[h],
                                    kn[h // rep] * (beta_b[h] * eg_b[h])], axis=1).astype(BF16))
              for h in heads_v]
        qe = [qn[h // rep] * eg_b[h] for h in heads_v]
        s = [state_scr[h] for h in heads_v]
        o_state, v_new = [[] for _ in heads_v], [[] for _ in heads_v]
        for c in range(2):
            cs = slice(c * CHUNK, (c + 1) * CHUNK)
            last = c * CHUNK + CHUNK - 1
            ws = [_dot(jnp.concatenate([uw[h][cs, hd:], qe[h][cs]], axis=0).astype(BF16), s[h].astype(BF16))
                  for h in heads_v]
            for h in heads_v:
                v_new[h].append((uw[h][cs, :hd] - ws[h][:CHUNK]).astype(BF16))
                o_state[h].append(ws[h][CHUNK:])
            g_last = [gc[last:last + 1, h:h + 1] for h in heads_v]
            s = [s[h] * jnp.exp(g_last[h])
                 + _dot_tn((kn[h // rep][cs] * jnp.exp(g_last[h] - g_b[h][cs])).astype(BF16), v_new[h][c])
                 for h in heads_v]
        for h in heads_v:
            state_scr[h] = s[h]
        o = [jnp.concatenate(o_state[h], axis=0) + _dot(intra[h], jnp.concatenate(v_new[h], axis=0))
             for h in heads_v]
        for h in heads_v:
            on = o[h] * lax.rsqrt(jnp.mean(o[h] * o[h], axis=-1, keepdims=True) + NORM_EPS) * normw_ref[...]
            gt = gate_ref[rs, h * hd:(h + 1) * hd].astype(F32)
            o_ref[rs, h * hd:(h + 1) * hd] = (on * _silu(gt)).astype(o_ref.dtype)

    npairs = tl // pr
    inputs = pair_inputs(0)
    for p in range(npairs):
        nxt = pair_inputs((p + 1) * pr) if p + 1 < npairs else None
        pair_body(p * pr, inputs)
        inputs = nxt
    tail = 2 * SUBLANES
    ext_scr[GDN_HIST - tail:GDN_HIST, :] = ext_scr[GDN_HIST + tl - tail:GDN_HIST + tl, :]


def _gdn_mixer(qkv, ab, gate, conv_w, a_log, dt_bias, norm_w, bsz, length, tl):
    v_heads = a_log.shape[0]
    qk_heads = v_heads // 2
    cw = qkv.shape[1]
    vw = v_heads * GDN_HEAD_DIM
    qkw = qk_heads * GDN_HEAD_DIM
    nt = length // tl
    gpar = jnp.zeros((2, LANES), F32)
    gpar = gpar.at[0, :v_heads].set(-jnp.exp(a_log.astype(F32)))
    gpar = gpar.at[1, :v_heads].set(dt_bias.astype(F32))
    kern = functools.partial(_gdn_kernel, tl=tl, qk_heads=qk_heads, v_heads=v_heads)
    const = lambda *shape: pl.BlockSpec(shape, lambda b, t: (0,) * len(shape))
    return pl.pallas_call(
        kern,
        grid=(bsz, nt),
        in_specs=[pl.BlockSpec((tl, cw), lambda b, t: (b * nt + t, 0)),
                  pl.BlockSpec((tl, LANES), lambda b, t: (b * nt + t, 0)),
                  pl.BlockSpec((tl, vw), lambda b, t: (b * nt + t, 0)),
                  const(CONV_WIDTH, cw), const(2, LANES), const(1, GDN_HEAD_DIM)],
        out_specs=pl.BlockSpec((tl, vw), lambda b, t: (b * nt + t, 0)),
        out_shape=jax.ShapeDtypeStruct((bsz * length, vw), BF16),
        scratch_shapes=[pltpu.VMEM((GDN_HIST + tl, cw), BF16),
                        pltpu.VMEM((tl, LANES), F32),
                        pltpu.VMEM((tl, LANES), F32),
                        pltpu.VMEM((v_heads, GDN_HEAD_DIM, GDN_HEAD_DIM), F32)],
        compiler_params=_cparams(("parallel", "arbitrary")),
        name="gdn_mixer",
    )(qkv, ab, gate, conv_w.astype(F32), gpar, norm_w.reshape(1, GDN_HEAD_DIM).astype(F32))


def _attn_out_kernel(ymix_ref, qx_ref, gx_ref, k_ref, v_ref, wa_ref, wb_ref, x_ref, fw_ref, o_ref,
                     *, final_norm):
    xw = qx_ref.shape[1]
    hd = xw // XATTN_HEADS
    q = qx_ref[...]
    ys = []
    for h in range(XATTN_HEADS):
        hs = slice(h * hd, (h + 1) * hd)
        s = _dot_nt(q[:, hs], k_ref[:, hs]) * (hd ** -0.5)
        s = s - jnp.max(s, axis=-1, keepdims=True)
        p = jnp.exp(s)
        p = p / jnp.sum(p, axis=-1, keepdims=True)
        ys.append(_dot(p.astype(BF16), v_ref[:, hs]))
    yx = jnp.concatenate(ys, axis=1) * _silu(gx_ref[...].astype(F32))
    acc = _dot(ymix_ref[...], wa_ref[...]) + _dot(yx.astype(BF16), wb_ref[...])
    xn = x_ref[...] + acc
    if final_norm:
        xn = xn * lax.rsqrt(jnp.mean(xn * xn, axis=-1, keepdims=True) + NORM_EPS) * fw_ref[...]
    o_ref[...] = xn


def _attn_out(ymix, qx, gx, k, v, w_out, x2d, final_w, bsz, length, tm, final_norm):
    t, d = x2d.shape
    mw = ymix.shape[1]
    xw = qx.shape[1]
    mem = k.shape[0] // bsz
    nt = length // tm
    wa = w_out[:mw].astype(BF16)
    wb = w_out[mw:].astype(BF16)
    kern = functools.partial(_attn_out_kernel, final_norm=final_norm)
    row = lambda width: pl.BlockSpec((tm, width), lambda b, i: (b * nt + i, 0))
    const = lambda *shape: pl.BlockSpec(shape, lambda b, i: (0,) * len(shape))
    return pl.pallas_call(
        kern,
        grid=(bsz, nt),
        in_specs=[row(mw), row(xw), row(xw),
                  pl.BlockSpec((mem, xw), lambda b, i: (b, 0)),
                  pl.BlockSpec((mem, xw), lambda b, i: (b, 0)),
                  const(mw, d), const(xw, d), row(d), const(1, d)],
        out_specs=row(d),
        out_shape=jax.ShapeDtypeStruct((t, d), F32),
        compiler_params=_cparams(("parallel", "parallel")),
        name="attn_out",
    )(ymix, qx, gx, k, v, wa, wb, x2d, final_w.reshape(1, d).astype(F32))


def _pick_tile(n, pref):
    t = pref
    while n % t:
        t //= 2
    return t


def kernel(x, mem, norm_w, w_out, mem_norm_w, w_mem_kv, s5_w_in, s5_lambda_re, s5_lambda_im, s5_log_step, s5_b_re, s5_b_im, s5_c_re, s5_c_im, s5_d, s5_w_glu, s5_b_glu, gdn_w_in, gdn_conv_w, gdn_a_log, gdn_dt_bias, gdn_norm_w, final_norm_w):
    bsz, length, d = x.shape
    mem_tokens = mem.shape[1]
    depth = norm_w.shape[0]
    mix_width = w_out.shape[1]
    xw = mix_width // 4
    tw = mix_width - xw
    v_heads = gdn_a_log.shape[1]
    qkw = (v_heads // 2) * GDN_HEAD_DIM

    t = bsz * length
    x2d = x.reshape(t, d)
    mem2d = mem.reshape(bsz * mem_tokens, d)
    tm_proj = _pick_tile(t, 512)
    tm_mem = _pick_tile(bsz * mem_tokens, 256)
    tl_s5 = _pick_tile(length, 256)
    tl_gdn = _pick_tile(length, 512)
    tm_out = _pick_tile(length, 1024)

    for i in range(depth):
        j = i // 2
        kmem, vmem = _norm_proj(mem2d, mem_norm_w[i], w_mem_kv[i].astype(BF16),
                                [(0, xw, BF16, 0), (xw, 2 * xw, BF16, 0)], tm_mem)
        if i % 2 == 0:
            u_blk, gate_mix, qx, gx = _norm_proj(
                x2d, norm_w[i], s5_w_in[j].astype(BF16),
                [(0, tw, BF16, S5_BLOCK_IN), (tw, 2 * tw, BF16, 0),
                 (2 * tw, 2 * tw + xw, BF16, 0), (2 * tw + xw, 2 * tw + 2 * xw, BF16, 0)], tm_proj)
            tables = _s5_tables(s5_lambda_re[j], s5_lambda_im[j], s5_log_step[j], s5_b_re[j], s5_b_im[j],
                                s5_c_re[j], s5_c_im[j], tl_s5)
            ymix = _s5_mixer(u_blk, gate_mix, tables, s5_d[j], s5_w_glu[j], s5_b_glu[j], bsz, length, tl_s5)
        else:
            w = gdn_w_in[j]
            o1 = 2 * qkw + tw
            o3 = o1 + 2 * v_heads
            o4 = o3 + tw
            ab_w = jnp.zeros((d, LANES), w.dtype).at[:, :2 * v_heads].set(w[:, o1:o3])
            wp = jnp.concatenate([w[:, :o1], w[:, o3:], ab_w], axis=1).astype(BF16)
            c_gate = o1
            c_qx = c_gate + tw
            c_gx = c_qx + xw
            c_ab = c_gx + xw
            qkv, gate_mix, qx, gx, ab = _norm_proj(
                x2d, norm_w[i], wp,
                [(0, o1, BF16, 0), (c_gate, c_qx, BF16, 0), (c_qx, c_gx, BF16, 0),
                 (c_gx, c_ab, BF16, 0), (c_ab, c_ab + LANES, F32, 0)], tm_proj)
            ymix = _gdn_mixer(qkv, ab, gate_mix, gdn_conv_w[j], gdn_a_log[j], gdn_dt_bias[j],
                              gdn_norm_w[j], bsz, length, tl_gdn)
        x2d = _attn_out(ymix, qx, gx, kmem, vmem, w_out[i], x2d, final_norm_w, bsz, length, tm_out,
                        final_norm=(i == depth - 1))
    return x2d.reshape(bsz, length, d)
```
